```python
import math
import jax
import jax.numpy as jnp
from jax import lax
import numpy as np

D_MODEL = 2048
BATCH = 1
SEQ = 8192
DEPTH = 1
DEC_BATCH = 32
DEC_SEQ = 4
PAST_LEN = 8192
PAGE_SIZE = 128

DN_HEADS = 16
DN_KEY_DIM = 128
DN_VAL_DIM = 128
CONV_WIDTH = 4
DN_CHUNK = 64
DN_CONV_CH = 2 * DN_HEADS * DN_KEY_DIM + DN_HEADS * DN_VAL_DIM
SA_HEADS = 16
SA_KV_HEADS = 4
SA_HEAD_DIM = 128
SA_GROUP = SA_HEADS // SA_KV_HEADS
IDX_HEADS = 16
IDX_DIM = 64
TOPK_MAX = 256
Q_BLOCK = 128
REL_BUCKETS = 32
REL_MAX_DIST = 128
N_EXPERTS = 32
TOP_K = 4
D_FF = 2048
SWIGLU_LIMIT = 7.0
SWIGLU_ALPHA = 1.702
MOE_BLOCK = 256

EPS = 1e-6
NEG_INF = -1e30

PROJ_SIZES = (
    DN_CONV_CH,
    DN_HEADS * DN_VAL_DIM,
    DN_HEADS,
    DN_HEADS,
    SA_HEADS * SA_HEAD_DIM,
    SA_KV_HEADS * SA_HEAD_DIM,
    SA_KV_HEADS * SA_HEAD_DIM,
    IDX_HEADS * IDX_DIM,
    IDX_DIM,
    IDX_HEADS,
    D_MODEL,
    D_MODEL,
)
D_PROJ = sum(PROJ_SIZES)
PROJ_OFFSETS = tuple(int(o) for o in np.cumsum(PROJ_SIZES)[:-1])

kernel_name = 'hybrid_gdn_dsa_moe_step'


def rms_norm(x, g):
    xf = x.astype(jnp.float32)
    y = xf * lax.rsqrt(jnp.mean(xf * xf, axis=-1, keepdims=True) + EPS)
    return (y * g.astype(jnp.float32)).astype(x.dtype)


def l2_norm(x):
    xf = x.astype(jnp.float32)
    return xf * lax.rsqrt(jnp.sum(xf * xf, axis=-1, keepdims=True) + EPS)


def chunk_gated_delta(q, k, v, beta, g, s0):
    bsz, seq_len = q.shape[:2]
    n_chunks = -(-seq_len // DN_CHUNK)
    pad = n_chunks * DN_CHUNK - seq_len

    def to_chunks(t):
        t = t.astype(jnp.float32)
        t = jnp.pad(t, [(0, 0), (0, pad)] + [(0, 0)] * (t.ndim - 2))
        t = t.reshape(bsz, n_chunks, DN_CHUNK, *t.shape[2:])
        return jnp.moveaxis(t, (1, 3), (0, 2))

    q = to_chunks(q) * (DN_KEY_DIM ** -0.5)
    k, v, beta, g = to_chunks(k), to_chunks(v), to_chunks(beta), to_chunks(g)
    gc = jnp.cumsum(g, axis=-1)
    incl = jnp.tril(jnp.ones((DN_CHUNK, DN_CHUNK), dtype=bool))
    strict = jnp.tril(jnp.ones((DN_CHUNK, DN_CHUNK), dtype=bool), -1)
    diff = gc[..., :, None] - gc[..., None, :]
    decay = jnp.where(incl, jnp.exp(jnp.where(incl, diff, 0.0)), 0.0)
    k_beta = k * beta[..., None]
    lower = jnp.where(strict, jnp.einsum('nbhid,nbhjd->nbhij', k_beta, k) * decay, 0.0)
    rhs = jnp.concatenate([v * beta[..., None], k_beta * jnp.exp(gc)[..., None]], axis=-1)
    sol = lax.linalg.triangular_solve(lower + jnp.eye(DN_CHUNK, dtype=jnp.float32), rhs,
                                      left_side=True, lower=True, unit_diagonal=True)
    u, w = sol[..., :DN_VAL_DIM], sol[..., DN_VAL_DIM:]
    attn = jnp.einsum('nbhid,nbhjd->nbhij', q, k) * decay
    g_last = gc[..., -1]

    def step(state, xs):
        qc, kc, uc, wc, gcc, ac, glc = xs
        v_new = uc - wc @ state
        out = (qc * jnp.exp(gcc)[..., None]) @ state + ac @ v_new
        k_dec = kc * jnp.exp(glc[..., None] - gcc)[..., None]
        state = state * jnp.exp(glc)[..., None, None] + jnp.einsum('bhcd,bhce->bhde', k_dec, v_new)
        return state, out

    s_fin, o = lax.scan(step, s0.astype(jnp.float32), (q, k, u, w, gc, attn, g_last))
    o = jnp.moveaxis(o, (0, 2), (1, 3)).reshape(bsz, n_chunks * DN_CHUNK, DN_HEADS, DN_VAL_DIM)
    return o[:, :seq_len], s_fin


def gated_delta_branch(qkv, z, a_raw, b_raw, conv_state, ssm_state, conv_w, a_log, dt_bias, g_out):
    bsz, seq_len, _ = qkv.shape
    xpad = jnp.concatenate([conv_state.astype(qkv.dtype), qkv], axis=1)
    conv = sum(xpad[:, j:j + seq_len] * conv_w[j] for j in range(CONV_WIDTH))
    new_conv = xpad[:, seq_len:]
    q, k, v = jnp.split(jax.nn.silu(conv), [DN_HEADS * DN_KEY_DIM, 2 * DN_HEADS * DN_KEY_DIM], axis=-1)
    q = l2_norm(q.reshape(bsz, seq_len, DN_HEADS, DN_KEY_DIM))
    k = l2_norm(k.reshape(bsz, seq_len, DN_HEADS, DN_KEY_DIM))
    v = v.reshape(bsz, seq_len, DN_HEADS, DN_VAL_DIM)
    beta = jax.nn.sigmoid(b_raw.astype(jnp.float32))
    g = -jnp.exp(a_log.astype(jnp.float32)) * jax.nn.softplus(a_raw.astype(jnp.float32) + dt_bias.astype(jnp.float32))
    o, new_ssm = chunk_gated_delta(q, k, v, beta, g, ssm_state)
    o = rms_norm(o.astype(qkv.dtype), g_out) * jax.nn.silu(z.reshape(bsz, seq_len, DN_HEADS, DN_VAL_DIM))
    return o.reshape(bsz, seq_len, DN_HEADS * DN_VAL_DIM), new_conv, new_ssm


def t5_bucket(dist):
    n = jnp.maximum(dist, 0)
    exact = REL_BUCKETS // 2
    log_ratio = jnp.log(jnp.maximum(n, exact).astype(jnp.float32) / exact) / math.log(REL_MAX_DIST / exact)
    large = jnp.minimum(exact + (log_ratio * (REL_BUCKETS - exact)).astype(jnp.int32), REL_BUCKETS - 1)
    return jnp.where(n < exact, n, large)


def indexer_topk(qi, wi, ki, q_pos, n_top):
    n_keys = ki.shape[1]
    dots = jnp.einsum('bthd,bsd->bths', qi, ki, preferred_element_type=jnp.float32) * (IDX_DIM ** -0.5)
    score = jnp.einsum('bth,bths->bts', wi.astype(jnp.float32), jax.nn.relu(dots))
    admissible = jnp.arange(n_keys, dtype=jnp.int32)[None, None, :] <= q_pos[None, :, None]
    score = jnp.where(admissible, score, -jnp.inf)
    _, idx = lax.top_k(score, n_top)
    return idx, idx <= q_pos[None, :, None]


def sparse_attend(q, k_sel, v_sel, q_pos, key_pos, valid, rel_bias):
    bsz, t = q.shape[:2]
    n_sel = k_sel.shape[2]
    qg = q.reshape(bsz, t, SA_KV_HEADS, SA_GROUP, SA_HEAD_DIM)
    logits = jnp.einsum('btgjd,btkgd->btgjk', qg, k_sel, preferred_element_type=jnp.float32) * (SA_HEAD_DIM ** -0.5)
    bias = rel_bias.astype(jnp.float32)[t5_bucket(q_pos[None, :, None] - key_pos)]
    bias = jnp.moveaxis(bias.reshape(bsz, t, n_sel, SA_KV_HEADS, SA_GROUP), 2, 4)
    logits = jnp.where(valid[:, :, None, None, :], logits + bias, NEG_INF)
    p = jax.nn.softmax(logits, axis=-1).astype(v_sel.dtype)
    o = jnp.einsum('btgjk,btkgd->btgjd', p, v_sel)
    return o.reshape(bsz, t, SA_HEADS * SA_HEAD_DIM)


def dsa_prompt(q, k, v, qi, wi, ki, rel_bias):
    bsz, seq_len = q.shape[:2]
    n_top = min(TOPK_MAX, seq_len // 4)
    n_blocks = seq_len // Q_BLOCK
    bi = jnp.arange(bsz)[:, None, None]

    def to_blocks(a):
        return jnp.swapaxes(a.reshape(bsz, n_blocks, Q_BLOCK, *a.shape[2:]), 0, 1)

    def block(args):
        qb, qib, wib, pos = args
        idx, valid = indexer_topk(qib, wib, ki, pos, n_top)
        return sparse_attend(qb, k[bi, idx], v[bi, idx], pos, idx, valid, rel_bias)

    pos = jnp.arange(seq_len, dtype=jnp.int32).reshape(n_blocks, Q_BLOCK)
    out = lax.map(block, (to_blocks(q), to_blocks(qi), to_blocks(wi), pos))
    return jnp.swapaxes(out, 0, 1).reshape(bsz, seq_len, SA_HEADS * SA_HEAD_DIM)


def dsa_sample(q, k_new, v_new, qi, wi, ki_new, cache_k, cache_v, cache_kidx, page_table, rel_bias):
    dbsz, t = q.shape[:2]
    past = page_table.shape[1] * PAGE_SIZE
    n_top = min(TOPK_MAX, (past + t) // 4)
    ki_past = cache_kidx[page_table].reshape(dbsz, past, IDX_DIM).astype(ki_new.dtype)
    ki_all = jnp.concatenate([ki_past, ki_new], axis=1)
    q_pos = past + jnp.arange(t, dtype=jnp.int32)
    idx, valid = indexer_topk(qi, wi, ki_all, q_pos, n_top)
    bi = jnp.arange(dbsz)[:, None, None]
    in_past = (idx < past)[..., None, None]
    p_idx = jnp.minimum(idx, past - 1)
    phys = page_table[bi, p_idx // PAGE_SIZE]
    slot = p_idx % PAGE_SIZE
    n_idx = jnp.clip(idx - past, 0, t - 1)
    k_sel = jnp.where(in_past, cache_k[phys, slot].astype(k_new.dtype), k_new[bi, n_idx])
    v_sel = jnp.where(in_past, cache_v[phys, slot].astype(v_new.dtype), v_new[bi, n_idx])
    return sparse_attend(q, k_sel, v_sel, q_pos, idx, valid, rel_bias)


def moe_ffn(h, w_router, b_router, w_up, b_up, w_down, b_down):
    n, d = h.shape
    logits = jnp.dot(h, w_router, preferred_element_type=jnp.float32) + b_router.astype(jnp.float32)
    top_logit, top_e = lax.top_k(logits, TOP_K)
    top_w = jax.nn.softmax(top_logit, axis=-1)
    n_assign = n * TOP_K
    blk = max(1, min(MOE_BLOCK, n_assign // N_EXPERTS))
    n_blocks = -(-n_assign // blk) + N_EXPERTS
    flat_e = top_e.reshape(-1)
    flat_tok = jnp.repeat(jnp.arange(n, dtype=jnp.int32), TOP_K)
    order = jnp.argsort(flat_e)
    sorted_e = flat_e[order]
    counts = jnp.bincount(flat_e, length=N_EXPERTS)
    start = jnp.cumsum(counts) - counts
    padded = (counts + blk - 1) // blk * blk
    padded_end = jnp.cumsum(padded)
    dest = padded_end[sorted_e] - padded[sorted_e] + jnp.arange(n_assign) - start[sorted_e]
    rows = n_blocks * blk
    row_tok = jnp.full((rows,), n, jnp.int32).at[dest].set(flat_tok[order])
    row_w = jnp.zeros((rows,), jnp.float32).at[dest].set(top_w.reshape(-1)[order])
    blk_e = jnp.minimum(jnp.sum(padded_end[None, :] <= (jnp.arange(n_blocks) * blk)[:, None], axis=1), N_EXPERTS - 1)
    h_pad = jnp.concatenate([h, jnp.zeros((1, d), h.dtype)], axis=0)

    def expert_block(acc, xs):
        tok, wt, e = xs
        gu = h_pad[tok] @ w_up[e] + b_up[e]
        gate = jnp.minimum(gu[:, :D_FF], SWIGLU_LIMIT)
        up = jnp.clip(gu[:, D_FF:], -SWIGLU_LIMIT, SWIGLU_LIMIT)
        out = ((up + 1) * gate * jax.nn.sigmoid(SWIGLU_ALPHA * gate)) @ w_down[e] + b_down[e]
        return acc.at[tok].add(out.astype(jnp.float32) * wt[:, None]), None

    y, _ = lax.scan(expert_block, jnp.zeros((n + 1, d), jnp.float32),
                    (row_tok.reshape(n_blocks, blk), row_w.reshape(n_blocks, blk), blk_e))
    return y[:n].astype(h.dtype)


def run_layer(x, c, conv_state, ssm_state, attend,
              w_ada, b_ada, g_norm1, w_in, conv_w, a_log, dt_bias, g_dn_out, g_q, g_k,
              w_o_dn, w_o_sa, w_out, g_norm2, w_router, b_router, w_up, b_up, w_down, b_down):
    bsz, seq_len, d = x.shape
    shift1, scale1, gate1, shift2, scale2, gate2 = jnp.split((c @ w_ada + b_ada)[:, None, :], 6, axis=-1)
    h = rms_norm(x, g_norm1) * (1 + scale1) + shift1
    (dn_qkv, dn_z, dn_a, dn_b, sa_q, sa_k, sa_v, ix_q, ix_k, ix_w, gate_dn, gate_sa) = jnp.split(h @ w_in, PROJ_OFFSETS, axis=-1)
    o_dn, new_conv, new_ssm = gated_delta_branch(dn_qkv, dn_z, dn_a, dn_b, conv_state, ssm_state,
                                                 conv_w, a_log, dt_bias, g_dn_out)
    q = rms_norm(sa_q.reshape(bsz, seq_len, SA_HEADS, SA_HEAD_DIM), g_q)
    k = rms_norm(sa_k.reshape(bsz, seq_len, SA_KV_HEADS, SA_HEAD_DIM), g_k)
    v = sa_v.reshape(bsz, seq_len, SA_KV_HEADS, SA_HEAD_DIM)
    qi = ix_q.reshape(bsz, seq_len, IDX_HEADS, IDX_DIM)
    o_sa = attend(q, k, v, qi, ix_w * (IDX_HEADS ** -0.5), ix_k)
    merged = jax.nn.sigmoid(gate_dn) * (o_dn @ w_o_dn) + jax.nn.sigmoid(gate_sa) * (o_sa @ w_o_sa)
    x = x + gate1 * (merged @ w_out)
    h2 = rms_norm(x, g_norm2) * (1 + scale2) + shift2
    x = x + gate2 * moe_ffn(h2.reshape(bsz * seq_len, d), w_router, b_router, w_up, b_up, w_down, b_down).reshape(bsz, seq_len, d)
    return x, k, v, ix_k, new_conv, new_ssm


def stack_layers(per_layer):
    return [jnp.stack(arrs) for arrs in zip(*per_layer)]


def setup_inputs(seed: int = 0) -> dict:
    key = jax.random.key(seed)
    ks = iter(jax.random.split(key, 40))
    f32 = jnp.float32

    def nrm(shape, scale):
        return scale * jax.random.normal(next(ks), shape, f32)

    def gain(shape):
        return 1.0 + nrm(shape, 0.02)

    n_pages = PAST_LEN // PAGE_SIZE
    n_pool = (DEC_BATCH * n_pages * 5) // 4
    page_table = jax.random.permutation(next(ks), n_pool)[: DEC_BATCH * n_pages].reshape(DEC_BATCH, n_pages).astype(jnp.int32)
    a_log = jnp.log(jax.random.uniform(next(ks), (DEPTH, DN_HEADS), f32, minval=1.0, maxval=16.0))
    dt = jnp.exp(jax.random.uniform(next(ks), (DEPTH, DN_HEADS), f32, minval=math.log(1e-3), maxval=math.log(1e-1)))
    dt_bias = dt + jnp.log(-jnp.expm1(-dt))
    return {
        'x_prompt': nrm((BATCH, SEQ, D_MODEL), 1.0),
        'x_sample': nrm((DEC_BATCH, DEC_SEQ, D_MODEL), 1.0),
        'cache_k': nrm((DEPTH, n_pool, PAGE_SIZE, SA_KV_HEADS, SA_HEAD_DIM), 1.0),
        'cache_v': nrm((DEPTH, n_pool, PAGE_SIZE, SA_KV_HEADS, SA_HEAD_DIM), 1.0),
        'cache_kidx': nrm((DEPTH, n_pool, PAGE_SIZE, IDX_DIM), 1.0),
        'state_conv': nrm((DEPTH, DEC_BATCH, CONV_WIDTH - 1, DN_CONV_CH), 1.0),
        'state_ssm': nrm((DEPTH, DEC_BATCH, DN_HEADS, DN_KEY_DIM, DN_VAL_DIM), DN_KEY_DIM ** -0.5),
        'page_table': page_table,
        'c_prompt': nrm((BATCH, D_MODEL), 1.0),
        'c_sample': nrm((DEC_BATCH, D_MODEL), 1.0),
        'rel_bias': nrm((REL_BUCKETS, SA_HEADS), 0.5),
        'w_ada': nrm((DEPTH, D_MODEL, 6 * D_MODEL), 0.5 * D_MODEL ** -0.5),
        'b_ada': nrm((DEPTH, 6 * D_MODEL), 0.02),
        'g_norm1': gain((DEPTH, D_MODEL)),
        'w_in': nrm((DEPTH, D_MODEL, D_PROJ), D_MODEL ** -0.5),
        'conv_w': nrm((DEPTH, CONV_WIDTH, DN_CONV_CH), CONV_WIDTH ** -0.5),
        'a_log': a_log,
        'dt_bias': dt_bias,
        'g_dn_out': gain((DEPTH, DN_VAL_DIM)),
        'g_q': gain((DEPTH, SA_HEAD_DIM)),
        'g_k': gain((DEPTH, SA_HEAD_DIM)),
        'w_o_dn': nrm((DEPTH, DN_HEADS * DN_VAL_DIM, D_MODEL), (DN_HEADS * DN_VAL_DIM) ** -0.5),
        'w_o_sa': nrm((DEPTH, SA_HEADS * SA_HEAD_DIM, D_MODEL), (SA_HEADS * SA_HEAD_DIM) ** -0.5),
        'w_out': nrm((DEPTH, D_MODEL, D_MODEL), D_MODEL ** -0.5),
        'g_norm2': gain((DEPTH, D_MODEL)),
        'w_router': nrm((DEPTH, D_MODEL, N_EXPERTS), D_MODEL ** -0.5),
        'b_router': nrm((DEPTH, N_EXPERTS), 0.01),
        'w_up': nrm((DEPTH, N_EXPERTS, D_MODEL, 2 * D_FF), D_MODEL ** -0.5),
        'b_up': nrm((DEPTH, N_EXPERTS, 2 * D_FF), 0.02),
        'w_down': nrm((DEPTH, N_EXPERTS, D_FF, D_MODEL), D_FF ** -0.5),
        'b_down': nrm((DEPTH, N_EXPERTS, D_MODEL), 0.02),
    }


def reference(x_prompt, x_sample, cache_k, cache_v, cache_kidx, state_conv, state_ssm, page_table,
              c_prompt, c_sample, rel_bias, w_ada, b_ada, g_norm1, w_in, conv_w, a_log, dt_bias,
              g_dn_out, g_q, g_k, w_o_dn, w_o_sa, w_out, g_norm2, w_router, b_router,
              w_up, b_up, w_down, b_down):
    layer_w = (w_ada, b_ada, g_norm1, w_in, conv_w, a_log, dt_bias, g_dn_out, g_q, g_k,
               w_o_dn, w_o_sa, w_out, g_norm2, w_router, b_router, w_up, b_up, w_down, b_down)
    bsz = x_prompt.shape[0]
    conv0 = jnp.zeros((bsz, CONV_WIDTH - 1, DN_CONV_CH), x_prompt.dtype)
    ssm0 = jnp.zeros((bsz, DN_HEADS, DN_KEY_DIM, DN_VAL_DIM), jnp.float32)

    def attend_prompt(q, k, v, qi, wi, ki):
        return dsa_prompt(q, k, v, qi, wi, ki, rel_bias)

    y_p, y_s = x_prompt, x_sample
    st_p, st_s = [], []
    for layer in range(DEPTH):
        lw = [w[layer] for w in layer_w]

        def attend_sample(q, k, v, qi, wi, ki, layer=layer):
            return dsa_sample(q, k, v, qi, wi, ki, cache_k[layer], cache_v[layer], cache_kidx[layer],
                              page_table, rel_bias)

        y_p, *new_p = run_layer(y_p, c_prompt, conv0, ssm0, attend_prompt, *lw)
        y_s, *new_s = run_layer(y_s, c_sample, state_conv[layer], state_ssm[layer], attend_sample, *lw)
        st_p.append(new_p)
        st_s.append(new_s)
    k_p, v_p, ki_p, conv_p, ssm_p = stack_layers(st_p)
    k_s, v_s, ki_s, conv_s, ssm_s = stack_layers(st_s)
    return (y_p, y_s, k_p, v_p, ki_p, conv_p, ssm_p, k_s, v_s, ki_s, conv_s, ssm_s)
```

```python
import functools
import math

import jax
import jax.numpy as jnp
import numpy as np
from jax import lax
from jax.experimental import pallas as pl
from jax.experimental.pallas import tpu as pltpu

D_MODEL = 2048
DN_HEADS = 16
DN_DIM = 128
CONV_WIDTH = 4
DN_CONV_CH = 3 * DN_HEADS * DN_DIM
SA_HEADS = 16
SA_KV_HEADS = 4
SA_GROUP = SA_HEADS // SA_KV_HEADS
SA_DIM = 128
IDX_HEADS = 16
IDX_DIM = 64
TOPK = 256
REL_BUCKETS = 32
REL_MAX_DIST = 128
N_EXPERTS = 32
TOP_K = 4
D_FF = 2048
SWIGLU_LIMIT = 7.0
SWIGLU_ALPHA = 1.702
MOE_BLOCK = 256
PAGE = 128
EPS = 1e-6

F32 = jnp.float32
BF16 = jnp.bfloat16
HI = lax.Precision.HIGHEST

VMEM_LIMIT = 56 * 1024 * 1024
TOK_TILE = 128

C_QKV, C_Z, C_SQ, C_SK, C_SV, C_IQ, C_GD, C_GS, C_BIG = 0, 6144, 8192, 10240, 10752, 11264, 12288, 14336, 16384
S_IK, S_A, S_B, S_IW = 0, 64, 80, 96


def _params(sem, vmem=VMEM_LIMIT):
    return pltpu.CompilerParams(dimension_semantics=sem, vmem_limit_bytes=vmem)


def _dot(a, b):
    return jnp.dot(a, b, preferred_element_type=F32)


def _dot_nt(a, b):
    return lax.dot_general(a, b, (((1,), (1,)), ((), ())), preferred_element_type=F32)


def _split_bf16(a):
    hi = a.astype(BF16)
    lo = (a - hi.astype(F32)).astype(BF16)
    return hi, lo


def _dot3(a, b):
    ah, al = _split_bf16(a)
    bh, bl = _split_bf16(b)
    return _dot(ah, bh) + (_dot(ah, bl) + _dot(al, bh))


def _ada_kernel(c_ref, w_ref, b_ref, o_ref):
    o_ref[...] = jnp.dot(c_ref[...], w_ref[...], precision=HI, preferred_element_type=F32) + b_ref[...]


def _adaln(c_all, w_ada, b_ada):
    r, d = c_all.shape
    n = w_ada.shape[1]
    tn = 1024
    return pl.pallas_call(
        _ada_kernel,
        grid=(n // tn,),
        in_specs=[pl.BlockSpec((r, d), lambda j: (0, 0)),
                  pl.BlockSpec((d, tn), lambda j: (0, j)),
                  pl.BlockSpec((1, tn), lambda j: (0, j))],
        out_specs=pl.BlockSpec((r, tn), lambda j: (0, j)),
        out_shape=jax.ShapeDtypeStruct((r, n), F32),
        compiler_params=_params(("parallel",)),
        name="adaln",
    )(c_all, w_ada, b_ada.reshape(1, n))


def _norm_mod_kernel(x_ref, g_ref, sh_ref, sc_ref, ws_ref, h_ref, s_ref):
    x = x_ref[...]
    y = x * lax.rsqrt(jnp.mean(x * x, axis=-1, keepdims=True) + EPS) * g_ref[...]
    h = y * (1.0 + sc_ref[...]) + sh_ref[...]
    h_ref[...] = h.astype(BF16)
    s_ref[...] = jnp.dot(h, ws_ref[...], precision=HI, preferred_element_type=F32)


def _mod_row_block(n_prompt_tiles):
    return lambda i: jnp.maximum(i - n_prompt_tiles + 1, 0)


def _norm_mod(x_all, g, mod_exp, w_small, n_prompt_tiles):
    t, d = x_all.shape
    rb = _mod_row_block(n_prompt_tiles)
    return pl.pallas_call(
        _norm_mod_kernel,
        grid=(t // TOK_TILE,),
        in_specs=[pl.BlockSpec((TOK_TILE, d), lambda i: (i, 0)),
                  pl.BlockSpec((1, d), lambda i: (0, 0)),
                  pl.BlockSpec((TOK_TILE, d), lambda i: (rb(i), 0)),
                  pl.BlockSpec((TOK_TILE, d), lambda i: (rb(i), 1)),
                  pl.BlockSpec((d, 128), lambda i: (0, 0))],
        out_specs=[pl.BlockSpec((TOK_TILE, d), lambda i: (i, 0)),
                   pl.BlockSpec((TOK_TILE, 128), lambda i: (i, 0))],
        out_shape=[jax.ShapeDtypeStruct((t, d), BF16), jax.ShapeDtypeStruct((t, 128), F32)],
        compiler_params=_params(("parallel",)),
        name="norm_mod",
    )(x_all, g.reshape(1, d), mod_exp, mod_exp, w_small)


def _mm_kernel(x_ref, w_ref, o_ref):
    o_ref[...] = _dot(x_ref[...], w_ref[...]).astype(o_ref.dtype)


def _row_tile(m, cap=1024):
    for tm in (1024, 896, 832, 768, 640, 512, 384, 256, 128):
        if tm <= cap and m % tm == 0:
            return tm
    raise ValueError(m)


def _matmul(x, w, out_dtype, tn=512):
    m, k = x.shape
    n = w.shape[1]
    tm = _row_tile(m)
    return pl.pallas_call(
        _mm_kernel,
        grid=(n // tn, m // tm),
        in_specs=[pl.BlockSpec((tm, k), lambda j, i: (i, 0)),
                  pl.BlockSpec((k, tn), lambda j, i: (0, j))],
        out_specs=pl.BlockSpec((tm, tn), lambda j, i: (i, j)),
        out_shape=jax.ShapeDtypeStruct((m, n), out_dtype),
        compiler_params=_params(("parallel", "parallel")),
        name="matmul",
    )(x, w)


def _sa_prep_kernel(q_ref, k_ref, v_ref, iq_ref, s_ref, gq_ref, gk_ref,
                    qn_ref, ko_ref, kb_ref, vo_ref, vt_ref, qi_ref, kio_ref, kib_ref, wit_ref):
    gq = gq_ref[...]
    gk = gk_ref[...]
    for h in range(SA_HEADS):
        x = q_ref[:, h * SA_DIM:(h + 1) * SA_DIM]
        y = x * lax.rsqrt(jnp.mean(x * x, axis=-1, keepdims=True) + EPS) * gq
        qn_ref[h] = (y * (SA_DIM ** -0.5)).astype(BF16)
    for g in range(SA_KV_HEADS):
        x = k_ref[:, g * SA_DIM:(g + 1) * SA_DIM]
        y = x * lax.rsqrt(jnp.mean(x * x, axis=-1, keepdims=True) + EPS) * gk
        ko_ref[:, g * SA_DIM:(g + 1) * SA_DIM] = y
        kb_ref[g] = y.astype(BF16)
        v = v_ref[:, g * SA_DIM:(g + 1) * SA_DIM]
        vt_ref[g, 0] = v.T.astype(BF16)
    vo_ref[...] = v_ref[...]
    for h in range(IDX_HEADS):
        qi_ref[h] = iq_ref[:, h * IDX_DIM:(h + 1) * IDX_DIM].astype(BF16)
    s = s_ref[...]
    ki = s[:, S_IK:S_IK + IDX_DIM]
    kio_ref[...] = ki
    kib_ref[...] = ki.astype(BF16)
    wit_ref[...] = s.T[S_IW:S_IW + IDX_HEADS, :] * ((IDX_HEADS ** -0.5) * (IDX_DIM ** -0.5))


def _sa_prep(p, small, g_q, g_k, row0, t, tm):
    nt = t // tm
    rb = row0 // tm
    col = lambda c0, w: (lambda i: (i + rb, c0 // w))
    out_shape = [
        jax.ShapeDtypeStruct((SA_HEADS, t, SA_DIM), BF16),
        jax.ShapeDtypeStruct((t, SA_KV_HEADS * SA_DIM), F32),
        jax.ShapeDtypeStruct((SA_KV_HEADS, t, SA_DIM), BF16),
        jax.ShapeDtypeStruct((t, SA_KV_HEADS * SA_DIM), F32),
        jax.ShapeDtypeStruct((SA_KV_HEADS, nt, SA_DIM, tm), BF16),
        jax.ShapeDtypeStruct((IDX_HEADS, t, IDX_DIM), BF16),
        jax.ShapeDtypeStruct((t, IDX_DIM), F32),
        jax.ShapeDtypeStruct((t, IDX_DIM), BF16),
        jax.ShapeDtypeStruct((IDX_HEADS, t), F32),
    ]
    out_specs = [
        pl.BlockSpec((SA_HEADS, tm, SA_DIM), lambda i: (0, i, 0)),
        pl.BlockSpec((tm, 512), lambda i: (i, 0)),
        pl.BlockSpec((SA_KV_HEADS, tm, SA_DIM), lambda i: (0, i, 0)),
        pl.BlockSpec((tm, 512), lambda i: (i, 0)),
        pl.BlockSpec((SA_KV_HEADS, 1, SA_DIM, tm), lambda i: (0, i, 0, 0)),
        pl.BlockSpec((IDX_HEADS, tm, IDX_DIM), lambda i: (0, i, 0)),
        pl.BlockSpec((tm, IDX_DIM), lambda i: (i, 0)),
        pl.BlockSpec((tm, IDX_DIM), lambda i: (i, 0)),
        pl.BlockSpec((IDX_HEADS, tm), lambda i: (0, i)),
    ]
    return pl.pallas_call(
        _sa_prep_kernel,
        grid=(nt,),
        in_specs=[pl.BlockSpec((tm, 2048), col(C_SQ, 2048)),
                  pl.BlockSpec((tm, 512), col(C_SK, 512)),
                  pl.BlockSpec((tm, 512), col(C_SV, 512)),
                  pl.BlockSpec((tm, 1024), col(C_IQ, 1024)),
                  pl.BlockSpec((tm, 128), lambda i: (i + rb, 0)),
                  pl.BlockSpec((1, SA_DIM), lambda i: (0, 0)),
                  pl.BlockSpec((1, SA_DIM), lambda i: (0, 0))],
        out_specs=out_specs,
        out_shape=out_shape,
        compiler_params=_params(("parallel",)),
        name="sa_prep",
    )(p, p, p, p, small, g_q.reshape(1, SA_DIM), g_k.reshape(1, SA_DIM))


DSA_TQ = 256
DSA_KS = 64
BISECT_ITERS = 32


def _t5_bucket(dist):
    n = jnp.maximum(dist, 0)
    exact = REL_BUCKETS // 2
    log_ratio = jnp.log(jnp.maximum(n, exact).astype(F32) / exact) / math.log(REL_MAX_DIST / exact)
    large = jnp.minimum(exact + (log_ratio * (REL_BUCKETS - exact)).astype(jnp.int32), REL_BUCKETS - 1)
    return jnp.where(n < exact, n, large)


def _topk_threshold(sc_ref, n_sub, tq):
    ks = DSA_KS
    inf = jnp.float32(jnp.inf)

    def mm_body(c, carry):
        mn, mx = carry
        blk = sc_ref[pl.ds(pl.multiple_of(c * ks, ks), ks), :].reshape(ks // 8, 8, tq)
        mx = jnp.maximum(mx, blk.max(axis=0))
        mn = jnp.minimum(mn, jnp.where(blk == -inf, inf, blk).min(axis=0))
        return mn, mx

    mn8, mx8 = lax.fori_loop(0, n_sub, mm_body, (jnp.full((8, tq), inf, F32), jnp.full((8, tq), -inf, F32)))
    lo = mn8.min(axis=0, keepdims=True)
    hi = mx8.max(axis=0, keepdims=True)

    def bis_body(_, carry):
        lo, hi = carry
        mid = 0.5 * (lo + hi)

        def cnt_body(c, acc):
            blk = sc_ref[pl.ds(pl.multiple_of(c * ks, ks), ks), :]
            return acc + jnp.where(blk >= mid, 1.0, 0.0).reshape(ks // 8, 8, tq).sum(axis=0)

        cnt = lax.fori_loop(0, n_sub, cnt_body, jnp.zeros((8, tq), F32)).sum(axis=0, keepdims=True)
        ok = cnt >= float(TOPK)
        return jnp.where(ok, mid, lo), jnp.where(ok, hi, mid)

    lo, hi = lax.fori_loop(0, BISECT_ITERS, bis_body, (lo, hi))
    return lo


def _dsa_prompt_kernel(bfar_ref, qi_ref, wit_ref, ki_ref, q_ref, k_ref, vt_ref, bias_ref,
                       o_ref, sc_ref, acc_ref, m_ref, l_ref):
    tq = tk = DSA_TQ
    ks = DSA_KS
    qb = pl.program_id(0)
    n_sub = (qb + 1) * (tk // ks)
    ninf = jnp.float32(-jnp.inf)
    t_idx = qb * tq + lax.broadcasted_iota(jnp.int32, (ks, tq), 1)

    def idx_body(st, carry):
        r0 = pl.multiple_of(st * ks, ks)
        ki_t = ki_ref[pl.ds(r0, ks), :]
        acc = jnp.zeros((ks, tq), F32)
        for h in range(IDX_HEADS):
            acc = acc + wit_ref[h:h + 1, :] * jnp.maximum(_dot_nt(ki_t, qi_ref[h]), 0.0)
        s_idx = r0 + lax.broadcasted_iota(jnp.int32, (ks, tq), 0)
        sc_ref[pl.ds(r0, ks), :] = jnp.where(s_idx <= t_idx, acc, ninf)
        return carry

    lax.fori_loop(0, n_sub, idx_body, 0)

    thr = _topk_threshold(sc_ref, n_sub, tq)

    def mask_body(c, carry):
        r0 = pl.multiple_of(c * ks, ks)
        sc_ref[pl.ds(r0, ks), :] = jnp.where(sc_ref[pl.ds(r0, ks), :] >= thr, 0.0, ninf)
        return carry

    lax.fori_loop(0, n_sub, mask_body, 0)

    m_ref[...] = jnp.full(m_ref.shape, ninf, F32)
    l_ref[...] = jnp.zeros(l_ref.shape, F32)
    acc_ref[...] = jnp.zeros(acc_ref.shape, F32)

    def tile(kt, bias_of_head):
        r0 = pl.multiple_of(kt * tk, tk)
        madd = sc_ref[pl.ds(r0, tk), :]

        def head_body(h, carry):
            g = h // SA_GROUP
            s = _dot_nt(k_ref[g, pl.ds(r0, tk), :], q_ref[h]) + bias_of_head(h) + madd
            m_old = m_ref[h]
            m_new = jnp.maximum(m_old, s.max(axis=0, keepdims=True))
            m_safe = jnp.where(m_new == ninf, 0.0, m_new)
            p = jnp.exp(s - m_safe)
            alpha = jnp.exp(m_old - m_safe)
            l_ref[h] = l_ref[h] * alpha + p.sum(axis=0, keepdims=True)
            acc_ref[h] = acc_ref[h] * alpha + _dot(vt_ref[g, kt], p.astype(BF16))
            m_ref[h] = m_new
            return carry

        lax.fori_loop(0, SA_HEADS, head_body, 0)

    def far_body(kt, carry):
        tile(kt, lambda h: bfar_ref[h])
        return carry

    lax.fori_loop(0, jnp.maximum(qb - 1, 0), far_body, 0)

    @pl.when(qb > 0)
    def _():
        tile(qb - 1, lambda h: bias_ref[h, 0:tk, :])

    tile(qb, lambda h: bias_ref[h, tk:2 * tk, :])

    for h in range(SA_HEADS):
        o_ref[:, h * SA_DIM:(h + 1) * SA_DIM] = (acc_ref[h] / l_ref[h]).T.astype(BF16)


def _resident(shape):
    nd = len(shape)
    return pl.BlockSpec(shape, lambda i: (0,) * nd, pipeline_mode=pl.Buffered(1))


def _dsa_prompt(qn, kb, vt, qi, kib, wit, rel_bias):
    t = qn.shape[1]
    tq = DSA_TQ
    assert t % tq == 0 and tq >= REL_MAX_DIST
    dist = jnp.arange(tq, dtype=jnp.int32)[None, :] + tq - jnp.arange(2 * tq, dtype=jnp.int32)[:, None]
    bias_near = jnp.moveaxis(rel_bias.astype(F32)[_t5_bucket(dist)], 2, 0)
    bias_far = rel_bias.astype(F32)[REL_BUCKETS - 1]
    return pl.pallas_call(
        _dsa_prompt_kernel,
        grid=(t // tq,),
        in_specs=[pl.BlockSpec(memory_space=pltpu.SMEM),
                  pl.BlockSpec((IDX_HEADS, tq, IDX_DIM), lambda i: (0, i, 0)),
                  pl.BlockSpec((IDX_HEADS, tq), lambda i: (0, i)),
                  _resident(kib.shape),
                  pl.BlockSpec((SA_HEADS, tq, SA_DIM), lambda i: (0, i, 0)),
                  _resident(kb.shape),
                  _resident(vt.shape),
                  _resident(bias_near.shape)],
        out_specs=pl.BlockSpec((tq, SA_HEADS * SA_DIM), lambda i: (i, 0)),
        out_shape=jax.ShapeDtypeStruct((t, SA_HEADS * SA_DIM), BF16),
        scratch_shapes=[pltpu.VMEM((t, tq), F32),
                        pltpu.VMEM((SA_HEADS, SA_DIM, tq), F32),
                        pltpu.VMEM((SA_HEADS, 1, tq), F32),
                        pltpu.VMEM((SA_HEADS, 1, tq), F32)],
        compiler_params=_params(("arbitrary",)),
        name="dsa_prompt",
    )(bias_far, qi, wit, kib, qn, kb, vt, bias_near)


def _dn_prep_kernel(x_ref, halo_ref, cw_ref, s_ref, alog_ref, dtb_ref, o_ref, beta_ref, g_ref,
                    *, tm, n_valid, zero_first_halo):
    x = x_ref[...]
    halo = halo_ref[...]
    if zero_first_halo:
        halo = jnp.where(pl.program_id(0) == 0, 0.0, halo)
    xcat = jnp.concatenate([halo, x], axis=0)
    cw = cw_ref[...]
    conv = x * cw[CONV_WIDTH - 1:CONV_WIDTH, :]
    for j in range(CONV_WIDTH - 1):
        lo = 8 - (CONV_WIDTH - 1) + j
        conv = conv + xcat[lo:lo + tm, :] * cw[j:j + 1, :]
    act = conv * jax.nn.sigmoid(conv)
    valid = None
    if n_valid < tm:
        valid = lax.broadcasted_iota(jnp.int32, (tm, 1), 0) < n_valid
    nqk = 2 * DN_HEADS
    for h in range(3 * DN_HEADS):
        a = act[:, h * DN_DIM:(h + 1) * DN_DIM]
        if h < nqk:
            a = a * lax.rsqrt(jnp.sum(a * a, axis=-1, keepdims=True) + EPS)
            if h < DN_HEADS:
                a = a * (DN_DIM ** -0.5)
        if valid is not None:
            a = jnp.where(valid, a, 0.0)
        o_ref[:, h * DN_DIM:(h + 1) * DN_DIM] = a
    s = s_ref[...]
    a_raw = s[:, S_A:S_A + DN_HEADS] + dtb_ref[...]
    softplus = jnp.maximum(a_raw, 0.0) + jnp.log1p(jnp.exp(-jnp.abs(a_raw)))
    g = -jnp.exp(alog_ref[...]) * softplus
    beta = jax.nn.sigmoid(s[:, S_B:S_B + DN_HEADS])
    if valid is not None:
        g = jnp.where(valid, g, 0.0)
        beta = jnp.where(valid, beta, 0.0)
    g_ref[...] = g
    beta_ref[...] = beta


def _dn_prep(x, x_col0, halo, halo_index, small, conv_w, a_log, dt_bias, *, tm, n_valid, zero_first_halo):
    t = small.shape[0]
    c = DN_CONV_CH
    cb = x_col0 // c
    kern = functools.partial(_dn_prep_kernel, tm=tm, n_valid=n_valid, zero_first_halo=zero_first_halo)
    return pl.pallas_call(
        kern,
        grid=(t // tm,),
        in_specs=[pl.BlockSpec((tm, c), lambda i: (i, cb)),
                  pl.BlockSpec((8, c), halo_index),
                  pl.BlockSpec((CONV_WIDTH, c), lambda i: (0, 0)),
                  pl.BlockSpec((tm, 128), lambda i: (i, 0)),
                  pl.BlockSpec((1, DN_HEADS), lambda i: (0, 0)),
                  pl.BlockSpec((1, DN_HEADS), lambda i: (0, 0))],
        out_specs=[pl.BlockSpec((tm, c), lambda i: (i, 0)),
                   pl.BlockSpec((tm, DN_HEADS), lambda i: (i, 0)),
                   pl.BlockSpec((tm, DN_HEADS), lambda i: (i, 0))],
        out_shape=[jax.ShapeDtypeStruct((t, c), F32),
                   jax.ShapeDtypeStruct((t, DN_HEADS), F32),
                   jax.ShapeDtypeStruct((t, DN_HEADS), F32)],
        compiler_params=_params(("parallel",)),
        name="dn_prep",
    )(x, halo, conv_w, small, a_log.reshape(1, DN_HEADS), dt_bias.reshape(1, DN_HEADS))


def _unit_lower_inverse(low, c):
    eye = (lax.broadcasted_iota(jnp.int32, (c, c), 0) == lax.broadcasted_iota(jnp.int32, (c, c), 1)).astype(F32)
    a = -low
    t = eye + a
    p = a
    for _ in range(int(math.log2(c)) - 1):
        p = _dot3(p, p)
        t = t + _dot3(t, p)
    return t


def _dn_scan_kernel(x_ref, z_ref, beta_ref, g_ref, gt_ref, s0_ref, gout_ref, o_ref, sfin_ref, s_ref, *, c):
    n = pl.program_id(1)

    @pl.when(n == 0)
    def _():
        s_ref[...] = s0_ref[0]

    row = lax.broadcasted_iota(jnp.int32, (c, c), 0)
    col = lax.broadcasted_iota(jnp.int32, (c, c), 1)
    incl = row >= col
    strict = row > col
    g_col = g_ref[...]
    g_row = gt_ref[:, 0, 0, :]
    gc_col = jnp.dot(incl.astype(F32), g_col, precision=HI, preferred_element_type=F32)
    gc_row = jnp.dot(g_row, (row <= col).astype(F32), precision=HI, preferred_element_type=F32)
    beta = beta_ref[...]
    gout = gout_ref[...]
    for h in range(DN_HEADS):
        q = x_ref[:, h * DN_DIM:(h + 1) * DN_DIM]
        k = x_ref[:, (DN_HEADS + h) * DN_DIM:(DN_HEADS + h + 1) * DN_DIM]
        v = x_ref[:, (2 * DN_HEADS + h) * DN_DIM:(2 * DN_HEADS + h + 1) * DN_DIM]
        gcc = gc_col[:, h:h + 1]
        gcr = gc_row[h:h + 1, :]
        bcol = beta[:, h:h + 1]
        decay = jnp.where(incl, jnp.exp(jnp.where(incl, gcc - gcr, 0.0)), 0.0)
        kb = k * bcol
        low = jnp.where(strict, _dot3(kb, k.T) * decay, 0.0)
        tinv = _unit_lower_inverse(low, c)
        rhs = jnp.concatenate([v * bcol, kb * jnp.exp(gcc)], axis=1)
        sol = _dot3(tinv, rhs)
        u = sol[:, :DN_DIM]
        w = sol[:, DN_DIM:]
        attn = jnp.where(incl, _dot3(q, k.T) * decay, 0.0)
        state = s_ref[h]
        v_new = u - _dot3(w, state)
        out = _dot3(q * jnp.exp(gcc), state) + _dot3(attn, v_new)
        g_last = gcc[c - 1:c, :]
        k_dec = k * jnp.exp(g_last - gcc)
        s_ref[h] = state * jnp.exp(g_last) + _dot3(k_dec.T, v_new)
        y = out * lax.rsqrt(jnp.mean(out * out, axis=-1, keepdims=True) + EPS) * gout
        z = z_ref[:, h * DN_DIM:(h + 1) * DN_DIM]
        o_ref[:, h * DN_DIM:(h + 1) * DN_DIM] = (y * (z * jax.nn.sigmoid(z))).astype(o_ref.dtype)

    @pl.when(n == pl.num_programs(1) - 1)
    def _():
        sfin_ref[0] = s_ref[...]


def _dn_scan(xc, z, z_col0, beta, g, s0, g_out, *, bsz, n_chunks, c):
    t = xc.shape[0]
    hdim = DN_HEADS * DN_DIM
    zb = z_col0 // hdim
    gt = g.T.reshape(DN_HEADS, bsz * n_chunks, 1, c)
    kern = functools.partial(_dn_scan_kernel, c=c)
    return pl.pallas_call(
        kern,
        grid=(bsz, n_chunks),
        in_specs=[pl.BlockSpec((c, DN_CONV_CH), lambda b, n: (b * n_chunks + n, 0)),
                  pl.BlockSpec((c, hdim), lambda b, n: (b * n_chunks + n, zb)),
                  pl.BlockSpec((c, DN_HEADS), lambda b, n: (b * n_chunks + n, 0)),
                  pl.BlockSpec((c, DN_HEADS), lambda b, n: (b * n_chunks + n, 0)),
                  pl.BlockSpec((DN_HEADS, 1, 1, c), lambda b, n: (0, b * n_chunks + n, 0, 0)),
                  pl.BlockSpec((1, DN_HEADS, DN_DIM, DN_DIM), lambda b, n: (b, 0, 0, 0)),
                  pl.BlockSpec((1, DN_DIM), lambda b, n: (0, 0))],
        out_specs=[pl.BlockSpec((c, hdim), lambda b, n: (b * n_chunks + n, 0)),
                   pl.BlockSpec((1, DN_HEADS, DN_DIM, DN_DIM), lambda b, n: (b, 0, 0, 0))],
        out_shape=[jax.ShapeDtypeStruct((t, hdim), BF16),
                   jax.ShapeDtypeStruct((bsz, DN_HEADS, DN_DIM, DN_DIM), F32)],
        scratch_shapes=[pltpu.VMEM((DN_HEADS, DN_DIM, DN_DIM), F32)],
        compiler_params=_params(("parallel", "arbitrary")),
        name="dn_scan",
    )(xc, z, beta, g, gt, s0, g_out.reshape(1, DN_DIM))


S1_PAGES = 8
S3_PAGES = 4
DEC_T = 4


def _dsa_s_scores_kernel(pt_ref, qi_ref, wsel_ref, kin_ref, *refs):
    pages = refs[:S1_PAGES]
    sc_ref, scn_ref = refs[S1_PAGES:]
    qi = qi_ref[0]
    wsel = wsel_ref[0]

    def score(keys_bf16):
        r = jnp.maximum(_dot_nt(qi, keys_bf16), 0.0)
        return jnp.dot(wsel, r, precision=HI, preferred_element_type=F32)

    for i in range(S1_PAGES):
        sc_ref[0, :, i * PAGE:(i + 1) * PAGE] = score(pages[i][0].astype(BF16))
    sn = score(kin_ref[0])
    t_idx = lax.broadcasted_iota(jnp.int32, sn.shape, 0)
    c_idx = lax.broadcasted_iota(jnp.int32, sn.shape, 1)
    scn_ref[0] = jnp.where((c_idx <= t_idx) & (c_idx < DEC_T), sn, -jnp.inf)


def _dsa_s_scores(page_table, qi_s, wsel, ki_new, cache_kidx):
    dbsz, n_pages = page_table.shape
    past = n_pages * PAGE
    steps = n_pages // S1_PAGES
    page_spec = lambda i: pl.BlockSpec((1, PAGE, IDX_DIM), lambda b, j, pt: (pt[b * n_pages + j * S1_PAGES + i], 0, 0))
    grid_spec = pltpu.PrefetchScalarGridSpec(
        num_scalar_prefetch=1,
        grid=(dbsz, steps),
        in_specs=[pl.BlockSpec((1,) + qi_s.shape[1:], lambda b, j, pt: (b, 0, 0)),
                  pl.BlockSpec((1,) + wsel.shape[1:], lambda b, j, pt: (b, 0, 0)),
                  pl.BlockSpec((1, PAGE, IDX_DIM), lambda b, j, pt: (b, 0, 0))]
                 + [page_spec(i) for i in range(S1_PAGES)],
        out_specs=[pl.BlockSpec((1, 8, S1_PAGES * PAGE), lambda b, j, pt: (b, 0, j)),
                   pl.BlockSpec((1, 8, PAGE), lambda b, j, pt: (b, 0, 0))],
    )
    return pl.pallas_call(
        _dsa_s_scores_kernel,
        grid_spec=grid_spec,
        out_shape=[jax.ShapeDtypeStruct((dbsz, 8, past), F32), jax.ShapeDtypeStruct((dbsz, 8, PAGE), F32)],
        compiler_params=_params(("parallel", "arbitrary")),
        name="dsa_sample_scores",
    )(page_table.reshape(-1), qi_s, wsel, ki_new, *([cache_kidx] * S1_PAGES))


def _dsa_s_mask_kernel(sc_ref, scn_ref, m_ref, mn_ref):
    inf = jnp.float32(jnp.inf)

    def row_min(s):
        return jnp.where(s == -inf, inf, s).min(axis=1, keepdims=True)

    lo = jnp.minimum(row_min(sc_ref[...]), row_min(scn_ref[...]))
    hi = jnp.maximum(sc_ref[...].max(axis=1, keepdims=True), scn_ref[...].max(axis=1, keepdims=True))

    def count(s, mid):
        return jnp.where(s >= mid, 1.0, 0.0).sum(axis=1, keepdims=True)

    def bis_body(_, carry):
        lo, hi = carry
        mid = 0.5 * (lo + hi)
        ok = (count(sc_ref[...], mid) + count(scn_ref[...], mid)) >= float(TOPK)
        return jnp.where(ok, mid, lo), jnp.where(ok, hi, mid)

    thr, _ = lax.fori_loop(0, BISECT_ITERS, bis_body, (lo, hi))
    m_ref[...] = jnp.where(sc_ref[...] >= thr, 0.0, -inf)
    mn_ref[...] = jnp.where(scn_ref[...] >= thr, 0.0, -inf)


def _dsa_s_mask(sc, scn):
    rows, past = sc.shape
    return pl.pallas_call(
        _dsa_s_mask_kernel,
        out_shape=[jax.ShapeDtypeStruct((rows, past), F32), jax.ShapeDtypeStruct((rows, PAGE), F32)],
        compiler_params=_params(None),
        name="dsa_sample_mask",
    )(sc, scn)


def _dsa_s_attn_kernel(pt_ref, q_ref, m_ref, mn_ref, kn_ref, vn_ref, bfar_ref, btail_ref, *refs,
                       n_steps):
    kp = refs[:S3_PAGES]
    vp = refs[S3_PAGES:2 * S3_PAGES]
    o_ref, acc_ref, mx_ref, l_ref = refs[2 * S3_PAGES:]
    j = pl.program_id(1)
    ninf = jnp.float32(-jnp.inf)
    rows = SA_GROUP * DEC_T

    def update(g, s, v_bf16, first):
        sl = slice(g * rows, (g + 1) * rows)
        m_old = jnp.full((rows, 1), ninf, F32) if first else mx_ref[sl, 0:1]
        m_new = jnp.maximum(m_old, s.max(axis=1, keepdims=True))
        m_safe = jnp.where(m_new == ninf, 0.0, m_new)
        p = jnp.exp(s - m_safe)
        pv = _dot(p.astype(BF16), v_bf16)
        if first:
            l_new = p.sum(axis=1, keepdims=True)
            acc_new = pv
        else:
            alpha = jnp.exp(m_old - m_safe)
            l_new = l_ref[sl, 0:1] * alpha + p.sum(axis=1, keepdims=True)
            acc_new = acc_ref[sl, :] * alpha + pv
        mx_ref[sl, :] = jnp.broadcast_to(m_new, (rows, 128))
        l_ref[sl, :] = jnp.broadcast_to(l_new, (rows, 128))
        acc_ref[sl, :] = acc_new

    def expand(mask_t):
        return jnp.concatenate([mask_t[0:DEC_T]] * SA_GROUP, axis=0)

    @pl.when(j == 0)
    def _():
        madd = expand(mn_ref[0])
        for g in range(SA_KV_HEADS):
            sl = slice(g * rows, (g + 1) * rows)
            k = kn_ref[0, :, g * SA_DIM:(g + 1) * SA_DIM]
            s = _dot_nt(q_ref[0, sl, :], k) + btail_ref[sl, PAGE:2 * PAGE] + madd
            update(g, s, vn_ref[0, :, g * SA_DIM:(g + 1) * SA_DIM], True)

    is_last = j == n_steps - 1
    madd = expand(m_ref[0])
    for g in range(SA_KV_HEADS):
        sl = slice(g * rows, (g + 1) * rows)
        k = jnp.concatenate([kp[i][0, :, g * SA_DIM:(g + 1) * SA_DIM] for i in range(S3_PAGES)], axis=0).astype(BF16)
        v = jnp.concatenate([vp[i][0, :, g * SA_DIM:(g + 1) * SA_DIM] for i in range(S3_PAGES)], axis=0).astype(BF16)
        bfar = bfar_ref[sl, :]
        bias = jnp.concatenate([bfar] * (S3_PAGES - 1) + [jnp.where(is_last, btail_ref[sl, 0:PAGE], bfar)], axis=1)
        s = _dot_nt(q_ref[0, sl, :], k) + bias + madd
        update(g, s, v, False)

    @pl.when(is_last)
    def _():
        o_ref[0] = acc_ref[...] / l_ref[...]


def _dsa_s_attn(page_table, q_s, madd, madd_new, k_new, v_new, bias_far_rows, bias_tail, cache_k, cache_v):
    dbsz, n_pages = page_table.shape
    n_steps = n_pages // S3_PAGES
    nrow = SA_HEADS * DEC_T
    kvd = SA_KV_HEADS * SA_DIM
    page_spec = lambda i: pl.BlockSpec((1, PAGE, kvd), lambda b, j, pt: (pt[b * n_pages + j * S3_PAGES + i], 0, 0))
    grid_spec = pltpu.PrefetchScalarGridSpec(
        num_scalar_prefetch=1,
        grid=(dbsz, n_steps),
        in_specs=[pl.BlockSpec((1, nrow, SA_DIM), lambda b, j, pt: (b, 0, 0)),
                  pl.BlockSpec((1, 8, S3_PAGES * PAGE), lambda b, j, pt: (b, 0, j)),
                  pl.BlockSpec((1, 8, PAGE), lambda b, j, pt: (b, 0, 0)),
                  pl.BlockSpec((1, PAGE, kvd), lambda b, j, pt: (b, 0, 0)),
                  pl.BlockSpec((1, PAGE, kvd), lambda b, j, pt: (b, 0, 0)),
                  pl.BlockSpec((nrow, PAGE), lambda b, j, pt: (0, 0)),
                  pl.BlockSpec((nrow, 2 * PAGE), lambda b, j, pt: (0, 0))]
                 + [page_spec(i) for i in range(S3_PAGES)] * 2,
        out_specs=pl.BlockSpec((1, nrow, SA_DIM), lambda b, j, pt: (b, 0, 0)),
        scratch_shapes=[pltpu.VMEM((nrow, SA_DIM), F32), pltpu.VMEM((nrow, 128), F32), pltpu.VMEM((nrow, 128), F32)],
    )
    return pl.pallas_call(
        functools.partial(_dsa_s_attn_kernel, n_steps=n_steps),
        grid_spec=grid_spec,
        out_shape=jax.ShapeDtypeStruct((dbsz, nrow, SA_DIM), F32),
        compiler_params=_params(("parallel", "arbitrary")),
        name="dsa_sample_attn",
    )(page_table.reshape(-1), q_s, madd, madd_new, k_new, v_new, bias_far_rows, bias_tail,
      *([cache_k] * S3_PAGES), *([cache_v] * S3_PAGES))


def _dsa_sample(qn, ko, vo, qi, kio, wit, cache_k, cache_v, cache_kidx, page_table, rel_bias):
    dbsz, n_pages = page_table.shape
    t = DEC_T
    past = n_pages * PAGE
    assert n_pages % S1_PAGES == 0 and n_pages % S3_PAGES == 0 and PAGE >= REL_MAX_DIST
    n_pool = cache_k.shape[0]
    qi_s = qi.reshape(IDX_HEADS, dbsz, t, IDX_DIM).transpose(1, 2, 0, 3).reshape(dbsz, t * IDX_HEADS, IDX_DIM)
    w_bth = wit.reshape(IDX_HEADS, dbsz, t).transpose(1, 2, 0)
    wsel = (w_bth[:, :, None, :] * jnp.eye(t, dtype=F32)[None, :, :, None]).reshape(dbsz, t, t * IDX_HEADS)
    wsel = jnp.pad(wsel, ((0, 0), (0, 8 - t), (0, 0)))
    pad_rows = lambda a: jnp.pad(a.reshape(dbsz, t, -1), ((0, 0), (0, PAGE - t), (0, 0))).astype(BF16)
    sc, scn = _dsa_s_scores(page_table, qi_s, wsel, pad_rows(kio), cache_kidx)
    madd, madd_new = _dsa_s_mask(sc.reshape(dbsz * 8, past), scn.reshape(dbsz * 8, PAGE))
    q_s = qn.reshape(SA_KV_HEADS, SA_GROUP, dbsz, t, SA_DIM).transpose(2, 0, 1, 3, 4).reshape(dbsz, SA_HEADS * t, SA_DIM)
    rb = rel_bias.astype(F32)
    head_of_row = jnp.repeat(jnp.arange(SA_HEADS), t)
    t_of_row = jnp.tile(jnp.arange(t, dtype=jnp.int32), SA_HEADS)
    cols = jnp.arange(PAGE, dtype=jnp.int32)
    dist_tail = jnp.concatenate([PAGE + t_of_row[:, None] - cols[None, :], t_of_row[:, None] - cols[None, :]], axis=1)
    bias_tail = rb[_t5_bucket(dist_tail), head_of_row[:, None]]
    bias_far_rows = jnp.broadcast_to(rb[REL_BUCKETS - 1][head_of_row][:, None], (SA_HEADS * t, PAGE))
    o = _dsa_s_attn(page_table, q_s, madd.reshape(dbsz, 8, past), madd_new.reshape(dbsz, 8, PAGE),
                    pad_rows(ko), pad_rows(vo), bias_far_rows, bias_tail,
                    cache_k.reshape(n_pool, PAGE, -1), cache_v.reshape(n_pool, PAGE, -1))
    return o.reshape(dbsz, SA_KV_HEADS, SA_GROUP, t, SA_DIM).transpose(0, 3, 1, 2, 4).reshape(dbsz * t, SA_HEADS * SA_DIM)


def _merge_kernel(odn_ref, osa_ref, gd_ref, gs_ref, w1_ref, w2_ref, o_ref):
    a = _dot(odn_ref[...], w1_ref[...])
    b = _dot(osa_ref[...], w2_ref[...])
    o_ref[...] = (jax.nn.sigmoid(gd_ref[...]) * a + jax.nn.sigmoid(gs_ref[...]) * b).astype(o_ref.dtype)


def _merge(o_dn, o_sa, p, w1, w2, tn=512):
    m, k = o_dn.shape
    n = w1.shape[1]
    tm = _row_tile(m)
    return pl.pallas_call(
        _merge_kernel,
        grid=(n // tn, m // tm),
        in_specs=[pl.BlockSpec((tm, k), lambda j, i: (i, 0)),
                  pl.BlockSpec((tm, k), lambda j, i: (i, 0)),
                  pl.BlockSpec((tm, tn), lambda j, i: (i, C_GD // tn + j)),
                  pl.BlockSpec((tm, tn), lambda j, i: (i, C_GS // tn + j)),
                  pl.BlockSpec((k, tn), lambda j, i: (0, j)),
                  pl.BlockSpec((k, tn), lambda j, i: (0, j))],
        out_specs=pl.BlockSpec((tm, tn), lambda j, i: (i, j)),
        out_shape=jax.ShapeDtypeStruct((m, n), BF16),
        compiler_params=_params(("parallel", "parallel")),
        name="merge",
    )(o_dn, o_sa, p, p, w1, w2)


def _pack_bf16_pair(lo, hi):
    lo_bits = pltpu.bitcast(lo.astype(BF16).astype(F32), jnp.uint32)
    hi_bits = pltpu.bitcast(hi.astype(BF16).astype(F32), jnp.uint32)
    return (hi_bits & jnp.uint32(0xFFFF0000)) | (lo_bits >> 16)


def _unpack_bf16_pair(w):
    lo = pltpu.bitcast(w << 16, F32)
    hi = pltpu.bitcast(w & jnp.uint32(0xFFFF0000), F32)
    return jnp.concatenate([lo, hi], axis=1).astype(BF16)


def _post_attn_kernel(x_ref, m_ref, g1_ref, sh_ref, sc_ref, gn_ref, wr_ref, br_ref, x1_ref, hp_ref, lg_ref):
    x1 = x_ref[...] + g1_ref[...] * m_ref[...]
    x1_ref[...] = x1
    y = x1 * lax.rsqrt(jnp.mean(x1 * x1, axis=-1, keepdims=True) + EPS) * gn_ref[...]
    h2 = y * (1.0 + sc_ref[...]) + sh_ref[...]
    half = h2.shape[1] // 2
    hp_ref[...] = _pack_bf16_pair(h2[:, :half], h2[:, half:])
    lg_ref[...] = lax.dot_general(wr_ref[...], h2, (((1,), (1,)), ((), ())), precision=HI,
                                  preferred_element_type=F32) + br_ref[...]


def _post_attn(x_all, m2, mod_exp, g_norm2, w_router, b_router, n_prompt_tiles):
    t, d = x_all.shape
    rb = _mod_row_block(n_prompt_tiles)
    tile = lambda c: pl.BlockSpec((TOK_TILE, d), lambda i: (rb(i), c))
    return pl.pallas_call(
        _post_attn_kernel,
        grid=(t // TOK_TILE,),
        in_specs=[pl.BlockSpec((TOK_TILE, d), lambda i: (i, 0)),
                  pl.BlockSpec((TOK_TILE, d), lambda i: (i, 0)),
                  tile(2), tile(3), tile(4),
                  pl.BlockSpec((1, d), lambda i: (0, 0)),
                  pl.BlockSpec((N_EXPERTS, d), lambda i: (0, 0)),
                  pl.BlockSpec((N_EXPERTS, 1), lambda i: (0, 0))],
        out_specs=[pl.BlockSpec((TOK_TILE, d), lambda i: (i, 0)),
                   pl.BlockSpec((TOK_TILE, d // 2), lambda i: (i, 0)),
                   pl.BlockSpec((N_EXPERTS, TOK_TILE), lambda i: (0, i))],
        out_shape=[jax.ShapeDtypeStruct((t, d), F32),
                   jax.ShapeDtypeStruct((t, d // 2), jnp.uint32),
                   jax.ShapeDtypeStruct((N_EXPERTS, t), F32)],
        compiler_params=_params(("parallel",)),
        name="post_attn",
    )(x_all, m2, mod_exp, mod_exp, mod_exp, g_norm2.reshape(1, d), w_router.T, b_router.reshape(N_EXPERTS, 1))


def _route_kernel(lg_ref, e_ref, w_ref, r_ref, cnt_ref, run_ref):
    @pl.when(pl.program_id(0) == 0)
    def _():
        run_ref[...] = jnp.zeros(run_ref.shape, F32)

    tt = TOK_TILE
    lg = lg_ref[...]
    e_iota = lax.broadcasted_iota(jnp.int32, lg.shape, 0)
    sels, tops = [], []
    for r in range(TOP_K):
        m = lg.max(axis=0, keepdims=True)
        e = jnp.where(lg == m, e_iota, N_EXPERTS).min(axis=0, keepdims=True)
        sel = e_iota == e
        lg = jnp.where(sel, -jnp.inf, lg)
        e_ref[r:r + 1, :] = e
        sels.append(sel)
        tops.append(m)
    ex = [jnp.exp(m - tops[0]) for m in tops]
    denom = ex[0] + ex[1] + ex[2] + ex[3]
    for r in range(TOP_K):
        w_ref[r:r + 1, :] = ex[r] / denom
    onehot = sels[0] | sels[1] | sels[2] | sels[3]
    oh = jnp.where(onehot, 1.0, 0.0).astype(BF16)
    before = (lax.broadcasted_iota(jnp.int32, (tt, tt), 0) < lax.broadcasted_iota(jnp.int32, (tt, tt), 1))
    prefix = _dot(oh, jnp.where(before, 1.0, 0.0).astype(BF16))
    total = _dot(oh, jnp.ones((tt, tt), BF16))
    base = run_ref[...] + prefix
    for r in range(TOP_K):
        r_ref[r:r + 1, :] = jnp.where(sels[r], base, 0.0).sum(axis=0, keepdims=True).astype(jnp.int32)
    run_ref[...] = run_ref[...] + total
    cnt_ref[...] = run_ref[...]


def _route(logits_t):
    e, t = logits_t.shape
    spec4 = pl.BlockSpec((TOP_K, TOK_TILE), lambda i: (0, i))
    return pl.pallas_call(
        _route_kernel,
        grid=(t // TOK_TILE,),
        in_specs=[pl.BlockSpec((e, TOK_TILE), lambda i: (0, i))],
        out_specs=[spec4, spec4, spec4, pl.BlockSpec((e, TOK_TILE), lambda i: (0, 0))],
        out_shape=[jax.ShapeDtypeStruct((TOP_K, t), jnp.int32),
                   jax.ShapeDtypeStruct((TOP_K, t), F32),
                   jax.ShapeDtypeStruct((TOP_K, t), jnp.int32),
                   jax.ShapeDtypeStruct((e, TOK_TILE), F32)],
        scratch_shapes=[pltpu.VMEM((e, TOK_TILE), F32)],
        compiler_params=_params(("arbitrary",)),
        name="route",
    )(logits_t)


def _dispatch_kernel(idx_ref, h_ref, o_ref):
    def body(r, carry):
        o_ref[pl.ds(r, 1), :] = h_ref[pl.ds(idx_ref[r], 1), :]
        return carry

    lax.fori_loop(0, MOE_BLOCK, body, 0, unroll=8)


def _dispatch(row_tok, h2p):
    rows = row_tok.shape[0]
    t, w = h2p.shape
    return pl.pallas_call(
        _dispatch_kernel,
        grid=(rows // MOE_BLOCK,),
        in_specs=[pl.BlockSpec((MOE_BLOCK,), lambda i: (i,), memory_space=pltpu.SMEM),
                  pl.BlockSpec((t, w), lambda i: (0, 0), pipeline_mode=pl.Buffered(1))],
        out_specs=pl.BlockSpec((MOE_BLOCK, w), lambda i: (i, 0)),
        out_shape=jax.ShapeDtypeStruct((rows, w), jnp.uint32),
        compiler_params=_params(("parallel",)),
        name="moe_dispatch",
    )(row_tok, h2p)


MOE_GROUP = 4
FF_TILE = 512
N_FF = D_FF // FF_TILE
ST_BLK, ST_F, ST_E, ST_J, ST_CAST, ST_VALID, ST_DONE = range(7)


def _expert_kernel(st_ref, dest_ref, x_ref, wg_ref, wu_ref, wd_ref, bg_ref, bu_ref, bd_ref, ys_ref,
                   wgb_ref, wub_ref, wdb_ref, acc_ref, stage_ref, sem_ref, inflight_ref):
    s = pl.program_id(0)
    f = st_ref[ST_F, s]
    j = st_ref[ST_J, s]

    def row_copy(slot, r, dst_row):
        return pltpu.make_async_copy(stage_ref.at[slot, pl.ds(r, 1)], ys_ref.at[pl.ds(dst_row, 1)], sem_ref.at[slot])

    def drain(slot):
        def body(r, carry):
            row_copy(slot, 0, 0).wait()
            return carry
        lax.fori_loop(0, inflight_ref[slot], body, 0)
        inflight_ref[slot] = 0

    @pl.when(s == 0)
    def _():
        inflight_ref[0] = 0
        inflight_ref[1] = 0

    @pl.when(st_ref[ST_CAST, s] == 1)
    def _():
        wgb_ref[...] = wg_ref[0].astype(BF16)
        wub_ref[...] = wu_ref[0].astype(BF16)
        wdb_ref[...] = wd_ref[0].astype(BF16)

    @pl.when(st_ref[ST_VALID, s] == 1)
    def _():
        x = _unpack_bf16_pair(x_ref[...])
        gate = jnp.minimum(_dot(x, wgb_ref[...]) + bg_ref[0], SWIGLU_LIMIT)
        up = jnp.clip(_dot(x, wub_ref[...]) + bu_ref[0], -SWIGLU_LIMIT, SWIGLU_LIMIT)
        act = (up + 1.0) * gate * jax.nn.sigmoid(SWIGLU_ALPHA * gate)
        part = _dot(act.astype(BF16), wdb_ref[...])

        @pl.when(f == 0)
        def _():
            acc_ref[j] = part

        @pl.when(f > 0)
        def _():
            acc_ref[j] = acc_ref[j] + part

        @pl.when(f == N_FF - 1)
        def _():
            slot = st_ref[ST_DONE, s] % 2
            drain(slot)
            stage_ref[slot] = acc_ref[j] + bd_ref[0]

            def body(r, n):
                dst = dest_ref[r]

                @pl.when(dst >= 0)
                def _():
                    row_copy(slot, r, dst).start()
                return n + jnp.where(dst >= 0, 1, 0)
            inflight_ref[slot] = lax.fori_loop(0, MOE_BLOCK, body, 0)

    @pl.when(s == pl.num_programs(0) - 1)
    def _():
        drain(0)
        drain(1)


def _experts(steps, dest_row, xg, w_up, b_up, w_down, b_down, n_out_rows):
    n_steps = steps.shape[1]
    rows, wpk = xg.shape
    d = w_down.shape[2]
    e = w_up.shape[0]
    grid_spec = pltpu.PrefetchScalarGridSpec(
        num_scalar_prefetch=1,
        grid=(n_steps,),
        in_specs=[pl.BlockSpec((MOE_BLOCK,), lambda s, st: (st[ST_BLK, s],), memory_space=pltpu.SMEM),
                  pl.BlockSpec((MOE_BLOCK, wpk), lambda s, st: (st[ST_BLK, s], 0)),
                  pl.BlockSpec((1, d, FF_TILE), lambda s, st: (st[ST_E, s], 0, st[ST_F, s])),
                  pl.BlockSpec((1, d, FF_TILE), lambda s, st: (st[ST_E, s], 0, N_FF + st[ST_F, s])),
                  pl.BlockSpec((1, FF_TILE, d), lambda s, st: (st[ST_E, s], st[ST_F, s], 0)),
                  pl.BlockSpec((1, 1, FF_TILE), lambda s, st: (st[ST_E, s], 0, st[ST_F, s])),
                  pl.BlockSpec((1, 1, FF_TILE), lambda s, st: (st[ST_E, s], 0, N_FF + st[ST_F, s])),
                  pl.BlockSpec((1, 1, d), lambda s, st: (st[ST_E, s], 0, 0))],
        out_specs=pl.BlockSpec(memory_space=pl.ANY),
        scratch_shapes=[pltpu.VMEM((d, FF_TILE), BF16), pltpu.VMEM((d, FF_TILE), BF16), pltpu.VMEM((FF_TILE, d), BF16),
                        pltpu.VMEM((MOE_GROUP, MOE_BLOCK, d), F32),
                        pltpu.VMEM((2, MOE_BLOCK, d), F32),
                        pltpu.SemaphoreType.DMA((2,)),
                        pltpu.SMEM((2,), jnp.int32)],
    )
    return pl.pallas_call(
        _expert_kernel,
        grid_spec=grid_spec,
        out_shape=jax.ShapeDtypeStruct((n_out_rows, d), F32),
        compiler_params=_params(("arbitrary",)),
        name="moe_experts",
    )(steps, dest_row, xg, w_up, w_up, w_down, b_up.reshape(e, 1, -1), b_up.reshape(e, 1, -1), b_down.reshape(e, 1, d))


def _moe_plan(top_e, rank, counts, t):
    blk = MOE_BLOCK
    n_assign = TOP_K * t
    n_blocks = -(-n_assign // blk) + N_EXPERTS
    rows = n_blocks * blk
    nblk_e = (counts + blk - 1) // blk
    blk_end = jnp.cumsum(nblk_e)
    blk_start = blk_end - nblk_e
    used = blk_end[-1]
    dest = blk_start[top_e] * blk + rank
    flat_slot_tok = (jnp.arange(TOP_K, dtype=jnp.int32)[:, None] * t + jnp.arange(t, dtype=jnp.int32)[None, :])
    row_tok = jnp.zeros((rows,), jnp.int32).at[dest.reshape(-1)].set(jnp.tile(jnp.arange(t, dtype=jnp.int32), TOP_K))
    dest_row = jnp.full((rows,), -1, jnp.int32).at[dest.reshape(-1)].set(flat_slot_tok.reshape(-1))
    b = jnp.arange(n_blocks, dtype=jnp.int32)
    e_b = jnp.minimum(jnp.sum(blk_end[None, :] <= b[:, None], axis=1), N_EXPERTS - 1).astype(jnp.int32)
    lb = b - blk_start[e_b]
    j_b = lb % MOE_GROUP
    gsize = jnp.minimum(MOE_GROUP, nblk_e[e_b] - (lb - j_b))
    valid_b = b < used
    fidx = jnp.arange(N_FF, dtype=jnp.int32)
    step_of = N_FF * (b - j_b)[:, None] + fidx[None, :] * gsize[:, None] + j_b[:, None]
    n_steps = N_FF * n_blocks
    step_of = jnp.where(valid_b[:, None], step_of, n_steps)
    def scat(vals, fill):
        return jnp.full((n_steps,), fill, jnp.int32).at[step_of.reshape(-1)].set(
            jnp.broadcast_to(vals, (n_blocks, N_FF)).reshape(-1).astype(jnp.int32), mode="drop")
    st_valid = scat(jnp.ones((n_blocks, 1), jnp.int32), 0)
    st_blk = scat(b[:, None], -1)
    st_f = scat(fidx[None, :], -1)
    st_e = scat(e_b[:, None], -1)
    st_j = scat(j_b[:, None], 0)
    st_cast = scat((j_b == 0)[:, None], 0)
    n_valid_steps = N_FF * used
    last_idx = jnp.maximum(n_valid_steps - 1, 0)
    pad = jnp.arange(n_steps) >= n_valid_steps
    fix = lambda a: jnp.where(pad, a[last_idx], a)
    st_blk, st_f, st_e = fix(st_blk), fix(st_f), fix(st_e)
    finishing = (st_valid == 1) & (st_f == N_FF - 1)
    st_done = jnp.cumsum(finishing.astype(jnp.int32)) - finishing.astype(jnp.int32)
    steps = jnp.stack([st_blk, st_f, st_e, st_j, st_cast, st_valid, st_done]).astype(jnp.int32)
    return steps, row_tok, dest_row, rows


def _final_kernel(x1_ref, g2_ref, w_ref, y0_ref, y1_ref, y2_ref, y3_ref, o_ref):
    w = w_ref[...]
    moe = (w[:, 0:1] * y0_ref[...] + w[:, 1:2] * y1_ref[...]) + (w[:, 2:3] * y2_ref[...] + w[:, 3:4] * y3_ref[...])
    o_ref[...] = x1_ref[...] + g2_ref[...] * moe


def _final(x1, mod_exp, top_w_t, ys, n_prompt_tiles):
    t, d = x1.shape
    rb = _mod_row_block(n_prompt_tiles)
    nt = t // TOK_TILE
    slot = lambda k: pl.BlockSpec((TOK_TILE, d), lambda i: (k * nt + i, 0))
    return pl.pallas_call(
        _final_kernel,
        grid=(nt,),
        in_specs=[pl.BlockSpec((TOK_TILE, d), lambda i: (i, 0)),
                  pl.BlockSpec((TOK_TILE, d), lambda i: (rb(i), 5)),
                  pl.BlockSpec((TOK_TILE, TOP_K), lambda i: (i, 0)),
                  slot(0), slot(1), slot(2), slot(3)],
        out_specs=pl.BlockSpec((TOK_TILE, d), lambda i: (i, 0)),
        out_shape=jax.ShapeDtypeStruct((t, d), F32),
        compiler_params=_params(("parallel",)),
        name="moe_combine",
    )(x1, mod_exp, top_w_t, ys, ys, ys, ys)


def _moe(h2p, logits_t, w_up, b_up, w_down, b_down):
    t = h2p.shape[0]
    top_e, top_w, rank, cnt = _route(logits_t)
    counts = cnt[:, 0].astype(jnp.int32)
    steps, row_tok, dest_row, rows = _moe_plan(top_e, rank, counts, t)
    xg = _dispatch(row_tok, h2p)
    ys = _experts(steps, dest_row, xg, w_up, b_up, w_down, b_down, TOP_K * t)
    return ys, top_w.T


PROJ_SIZES = (DN_CONV_CH, DN_HEADS * DN_DIM, DN_HEADS, DN_HEADS, SA_HEADS * SA_DIM, SA_KV_HEADS * SA_DIM,
              SA_KV_HEADS * SA_DIM, IDX_HEADS * IDX_DIM, IDX_DIM, IDX_HEADS, D_MODEL, D_MODEL)


def _split_w_in(w_in):
    ends = np.cumsum(PROJ_SIZES)
    seg = [w_in[:, int(e - s):int(e)] for s, e in zip(PROJ_SIZES, ends)]
    (dn_qkv, dn_z, dn_a, dn_b, sa_q, sa_k, sa_v, ix_q, ix_k, ix_w, gate_dn, gate_sa) = seg
    w_big = jnp.concatenate([dn_qkv, dn_z, sa_q, sa_k, sa_v, ix_q, gate_dn, gate_sa], axis=1).astype(BF16)
    pad = jnp.zeros((w_in.shape[0], 128 - (IDX_DIM + 2 * DN_HEADS + IDX_HEADS)), w_in.dtype)
    w_small = jnp.concatenate([ix_k, dn_a, dn_b, ix_w, pad], axis=1)
    return w_big, w_small


def _pad_seq(a, bsz, t, t_pad, front=0):
    a = a.reshape(bsz, t, -1)
    return jnp.pad(a, ((0, 0), (front, t_pad - t - front), (0, 0))).reshape(bsz * t_pad, -1)


def kernel(x_prompt, x_sample, cache_k, cache_v, cache_kidx, state_conv, state_ssm, page_table, c_prompt, c_sample,
           rel_bias, w_ada, b_ada, g_norm1, w_in, conv_w, a_log, dt_bias, g_dn_out, g_q, g_k, w_o_dn, w_o_sa, w_out,
           g_norm2, w_router, b_router, w_up, b_up, w_down, b_down):
    assert w_ada.shape[0] == 1 and x_prompt.shape[0] == 1
    d = D_MODEL
    tp = x_prompt.shape[1]
    dbsz, dt = x_sample.shape[:2]
    ts = dbsz * dt
    assert dt == DEC_T and ts == TOK_TILE and tp % DSA_TQ == 0
    npt = tp // TOK_TILE
    x_all = jnp.concatenate([x_prompt.reshape(tp, d), x_sample.reshape(ts, d)], axis=0)

    c_all = jnp.concatenate([c_prompt, c_sample], axis=0)
    c_all = jnp.pad(c_all, ((0, -c_all.shape[0] % 8), (0, 0)))
    mod = _adaln(c_all, w_ada[0], b_ada[0])
    mod_exp = jnp.concatenate([jnp.broadcast_to(mod[0:1], (TOK_TILE, mod.shape[1])),
                               jnp.repeat(mod[1:1 + dbsz], dt, axis=0)], axis=0)

    w_big, w_small = _split_w_in(w_in[0])
    h, small = _norm_mod(x_all, g_norm1[0], mod_exp, w_small, npt)
    p = _matmul(h, w_big, F32)

    qn, ko_p, kb, vo_p, vt, qi, kio_p, kib, wit = _sa_prep(p, small, g_q[0], g_k[0], 0, tp, DSA_TQ)
    o_sa_p = _dsa_prompt(qn, kb, vt, qi, kib, wit, rel_bias)
    qn_s, ko_s, _, vo_s, _, qi_s, kio_s, _, wit_s = _sa_prep(p, small, g_q[0], g_k[0], tp, ts, TOK_TILE)
    o_sa_s = _dsa_sample(qn_s, ko_s, vo_s, qi_s, kio_s, wit_s, cache_k[0], cache_v[0], cache_kidx[0], page_table, rel_bias)

    tm = 256
    xc, beta, g = _dn_prep(p, C_QKV, p, lambda i: (jnp.maximum(i * (tm // 8) - 1, 0), 0), small[:tp],
                           conv_w[0], a_log[0], dt_bias[0], tm=tm, n_valid=tm, zero_first_halo=True)
    chunk = 64
    s0_p = jnp.zeros((1,) + state_ssm.shape[2:], F32)
    o_dn_p, ssm_p = _dn_scan(xc, p, C_Z, beta, g, s0_p, g_dn_out[0], bsz=1, n_chunks=tp // chunk, c=chunk)
    qkv_s = p[tp:, C_QKV:C_QKV + DN_CONV_CH]
    x_s = _pad_seq(qkv_s, dbsz, dt, 8)
    halo_s = _pad_seq(state_conv[0].reshape(dbsz * (CONV_WIDTH - 1), -1), dbsz, CONV_WIDTH - 1, 8, front=8 - (CONV_WIDTH - 1))
    xc_s, beta_s, g_s = _dn_prep(x_s, 0, halo_s, lambda i: (i, 0), _pad_seq(small[tp:], dbsz, dt, 8),
                                 conv_w[0], a_log[0], dt_bias[0], tm=8, n_valid=dt, zero_first_halo=False)
    z_s = _pad_seq(p[tp:, C_Z:C_Z + DN_HEADS * DN_DIM], dbsz, dt, 8)
    o_dn_s, ssm_s = _dn_scan(xc_s, z_s, 0, beta_s, g_s, state_ssm[0], g_dn_out[0], bsz=dbsz, n_chunks=1, c=8)
    o_dn_s = o_dn_s.reshape(dbsz, 8, -1)[:, :dt].reshape(ts, -1)
    conv_p = p[tp - (CONV_WIDTH - 1):tp, C_QKV:C_QKV + DN_CONV_CH]
    conv_s = jnp.concatenate([state_conv[0], qkv_s.reshape(dbsz, dt, -1)], axis=1)[:, -(CONV_WIDTH - 1):]

    o_dn = jnp.concatenate([o_dn_p, o_dn_s], axis=0)
    o_sa = jnp.concatenate([o_sa_p, o_sa_s.astype(BF16)], axis=0)
    merged = _merge(o_dn, o_sa, p, w_o_dn[0].astype(BF16), w_o_sa[0].astype(BF16))
    m2 = _matmul(merged, w_out[0].astype(BF16), F32)
    x1, h2p, logits_t = _post_attn(x_all, m2, mod_exp, g_norm2[0], w_router[0], b_router[0], npt)
    ys, top_w_t = _moe(h2p, logits_t, w_up[0], b_up[0], w_down[0], b_down[0])
    y_all = _final(x1, mod_exp, top_w_t, ys, npt)

    kv = (SA_KV_HEADS, SA_DIM)
    return (y_all[:tp].reshape(x_prompt.shape), y_all[tp:].reshape(x_sample.shape),
            ko_p.reshape(1, 1, tp, *kv), vo_p.reshape(1, 1, tp, *kv), kio_p.reshape(1, 1, tp, IDX_DIM),
            conv_p.reshape(1, 1, CONV_WIDTH - 1, DN_CONV_CH), ssm_p.reshape(1, *ssm_p.shape),
            ko_s.reshape(1, dbsz, dt, *kv), vo_s.reshape(1, dbsz, dt, *kv), kio_s.reshape(1, dbsz, dt, IDX_DIM),
            conv_s.reshape(1, dbsz, CONV_WIDTH - 1, DN_CONV_CH), ssm_s.reshape(1, *ssm_s.shape))
```

```python
import functools
import math

import jax
import jax.numpy as jnp
import numpy as np
from jax import lax
from jax.experimental import pallas as pl
from jax.experimental.pallas import tpu as pltpu

D_MODEL = 2048
DN_HEADS = 16
DN_DIM = 128
CONV_WIDTH = 4
DN_CONV_CH = 3 * DN_HEADS * DN_DIM
SA_HEADS = 16
SA_KV_HEADS = 4
SA_GROUP = SA_HEADS // SA_KV_HEADS
SA_DIM = 128
IDX_HEADS = 16
IDX_DIM = 64
TOPK = 256
REL_BUCKETS = 32
REL_MAX_DIST = 128
N_EXPERTS = 32
TOP_K = 4
D_FF = 2048
SWIGLU_LIMIT = 7.0
SWIGLU_ALPHA = 1.702
MOE_BLOCK = 256
PAGE = 128
EPS = 1e-6

F32 = jnp.float32
BF16 = jnp.bfloat16
HI = lax.Precision.HIGHEST

VMEM_LIMIT = 56 * 1024 * 1024
TOK_TILE = 128

C_QKV, C_Z, C_SQ, C_SK, C_SV, C_IQ, C_GD, C_GS, C_BIG = 0, 6144, 8192, 10240, 10752, 11264, 12288, 14336, 16384
S_IK, S_A, S_B, S_IW = 0, 64, 80, 96


def _params(sem, vmem=VMEM_LIMIT):
    return pltpu.CompilerParams(dimension_semantics=sem, vmem_limit_bytes=vmem)


def _dot(a, b):
    return jnp.dot(a, b, preferred_element_type=F32)


def _dot_nt(a, b):
    return lax.dot_general(a, b, (((1,), (1,)), ((), ())), preferred_element_type=F32)


def _split_bf16(a):
    hi = a.astype(BF16)
    lo = (a - hi.astype(F32)).astype(BF16)
    return hi, lo


def _dot3(a, b):
    ah, al = _split_bf16(a)
    bh, bl = _split_bf16(b)
    return _dot(ah, bh) + (_dot(ah, bl) + _dot(al, bh))


def _ada_kernel(c_ref, w_ref, b_ref, o_ref):
    o_ref[...] = jnp.dot(c_ref[...], w_ref[...], precision=HI, preferred_element_type=F32) + b_ref[...]


def _adaln(c_all, w_ada, b_ada):
    r, d = c_all.shape
    n = w_ada.shape[1]
    tn = 1024
    return pl.pallas_call(
        _ada_kernel,
        grid=(n // tn,),
        in_specs=[pl.BlockSpec((r, d), lambda j: (0, 0)),
                  pl.BlockSpec((d, tn), lambda j: (0, j)),
                  pl.BlockSpec((1, tn), lambda j: (0, j))],
        out_specs=pl.BlockSpec((r, tn), lambda j: (0, j)),
        out_shape=jax.ShapeDtypeStruct((r, n), F32),
        compiler_params=_params(("parallel",)),
        name="adaln",
    )(c_all, w_ada, b_ada.reshape(1, n))


def _norm_mod_kernel(xp_ref, xs_ref, g_ref, sh_ref, sc_ref, ws_ref, h_ref, s_ref, *, n_prompt_tiles):
    x = jnp.where(pl.program_id(0) < n_prompt_tiles, xp_ref[...], xs_ref[...])
    y = x * lax.rsqrt(jnp.mean(x * x, axis=-1, keepdims=True) + EPS) * g_ref[...]
    h = y * (1.0 + sc_ref[...]) + sh_ref[...]
    h_ref[...] = h.astype(BF16)
    s_ref[...] = lax.dot_general(h, ws_ref[...], (((1,), (1,)), ((), ())), precision=HI, preferred_element_type=F32)


def _mod_row_block(n_prompt_tiles):
    return lambda i: jnp.maximum(i - n_prompt_tiles + 1, 0)


def _token_specs(d, n_prompt_tiles):
    last = n_prompt_tiles - 1
    return [pl.BlockSpec((TOK_TILE, d), lambda i: (jnp.minimum(i, last), 0)),
            pl.BlockSpec((TOK_TILE, d), lambda i: (jnp.maximum(i - n_prompt_tiles, 0), 0))]


def _norm_mod(x_p, x_s, g, mod_exp, w_small_t, n_prompt_tiles):
    d = x_p.shape[1]
    t = x_p.shape[0] + x_s.shape[0]
    rb = _mod_row_block(n_prompt_tiles)
    return pl.pallas_call(
        functools.partial(_norm_mod_kernel, n_prompt_tiles=n_prompt_tiles),
        grid=(t // TOK_TILE,),
        in_specs=_token_specs(d, n_prompt_tiles) + [
            pl.BlockSpec((1, d), lambda i: (0, 0)),
            pl.BlockSpec((TOK_TILE, d), lambda i: (rb(i), 0)),
            pl.BlockSpec((TOK_TILE, d), lambda i: (rb(i), 1)),
            pl.BlockSpec((128, d), lambda i: (0, 0))],
        out_specs=[pl.BlockSpec((TOK_TILE, d), lambda i: (i, 0)),
                   pl.BlockSpec((TOK_TILE, 128), lambda i: (i, 0))],
        out_shape=[jax.ShapeDtypeStruct((t, d), BF16), jax.ShapeDtypeStruct((t, 128), F32)],
        compiler_params=_params(("parallel",)),
        name="norm_mod",
    )(x_p, x_s, g.reshape(1, d), mod_exp, mod_exp, w_small_t)


def _mm_kernel(x_ref, w_ref, o_ref):
    o_ref[...] = _dot(x_ref[...], w_ref[...]).astype(o_ref.dtype)


def _mm_nt_kernel(x_ref, wt_ref, o_ref):
    o_ref[...] = _dot_nt(x_ref[...], wt_ref[...]).astype(o_ref.dtype)


def _row_tile(m, cap=1024):
    for tm in (1024, 896, 832, 768, 640, 512, 384, 256, 128):
        if tm <= cap and m % tm == 0:
            return tm
    raise ValueError(m)


def _matmul(x, w, out_dtype, tn=512, w_transposed=False):
    m, k = x.shape
    n = w.shape[0] if w_transposed else w.shape[1]
    tm = _row_tile(m)
    w_spec = pl.BlockSpec((tn, k), lambda j, i: (j, 0)) if w_transposed else pl.BlockSpec((k, tn), lambda j, i: (0, j))
    return pl.pallas_call(
        _mm_nt_kernel if w_transposed else _mm_kernel,
        grid=(n // tn, m // tm),
        in_specs=[pl.BlockSpec((tm, k), lambda j, i: (i, 0)), w_spec],
        out_specs=pl.BlockSpec((tm, tn), lambda j, i: (i, j)),
        out_shape=jax.ShapeDtypeStruct((m, n), out_dtype),
        compiler_params=_params(("parallel", "parallel")),
        name="matmul",
    )(x, w)


def _sa_prep_kernel(q_ref, k_ref, v_ref, iq_ref, s_ref, gq_ref, gk_ref,
                    qn_ref, ko_ref, kb_ref, vo_ref, vt_ref, qi_ref, kio_ref, kib_ref, wit_ref):
    gq = gq_ref[...]
    gk = gk_ref[...]
    for h in range(SA_HEADS):
        x = q_ref[:, h * SA_DIM:(h + 1) * SA_DIM]
        y = x * lax.rsqrt(jnp.mean(x * x, axis=-1, keepdims=True) + EPS) * gq
        qn_ref[h] = (y * (SA_DIM ** -0.5 * LOG2E)).astype(BF16)
    for g in range(SA_KV_HEADS):
        x = k_ref[:, g * SA_DIM:(g + 1) * SA_DIM]
        y = x * lax.rsqrt(jnp.mean(x * x, axis=-1, keepdims=True) + EPS) * gk
        ko_ref[:, g * SA_DIM:(g + 1) * SA_DIM] = y
        kb_ref[g] = y.astype(BF16)
        v = v_ref[:, g * SA_DIM:(g + 1) * SA_DIM]
        vt_ref[g, 0] = v.T.astype(BF16)
    vo_ref[...] = v_ref[...]
    for h in range(IDX_HEADS):
        qi_ref[h] = iq_ref[:, h * IDX_DIM:(h + 1) * IDX_DIM].astype(BF16)
    s = s_ref[...]
    ki = s[:, S_IK:S_IK + IDX_DIM]
    kio_ref[...] = ki
    kib_ref[...] = ki.astype(BF16)
    wit_ref[...] = s.T[S_IW:S_IW + IDX_HEADS, :] * ((IDX_HEADS ** -0.5) * (IDX_DIM ** -0.5))


def _sa_prep(p, small, g_q, g_k, row0, t, tm):
    nt = t // tm
    rb = row0 // tm
    col = lambda c0, w: (lambda i: (i + rb, c0 // w))
    out_shape = [
        jax.ShapeDtypeStruct((SA_HEADS, t, SA_DIM), BF16),
        jax.ShapeDtypeStruct((t, SA_KV_HEADS * SA_DIM), F32),
        jax.ShapeDtypeStruct((SA_KV_HEADS, t, SA_DIM), BF16),
        jax.ShapeDtypeStruct((t, SA_KV_HEADS * SA_DIM), F32),
        jax.ShapeDtypeStruct((SA_KV_HEADS, nt, SA_DIM, tm), BF16),
        jax.ShapeDtypeStruct((IDX_HEADS, t, IDX_DIM), BF16),
        jax.ShapeDtypeStruct((t, IDX_DIM), F32),
        jax.ShapeDtypeStruct((t, IDX_DIM), BF16),
        jax.ShapeDtypeStruct((IDX_HEADS, t), F32),
    ]
    out_specs = [
        pl.BlockSpec((SA_HEADS, tm, SA_DIM), lambda i: (0, i, 0)),
        pl.BlockSpec((tm, 512), lambda i: (i, 0)),
        pl.BlockSpec((SA_KV_HEADS, tm, SA_DIM), lambda i: (0, i, 0)),
        pl.BlockSpec((tm, 512), lambda i: (i, 0)),
        pl.BlockSpec((SA_KV_HEADS, 1, SA_DIM, tm), lambda i: (0, i, 0, 0)),
        pl.BlockSpec((IDX_HEADS, tm, IDX_DIM), lambda i: (0, i, 0)),
        pl.BlockSpec((tm, IDX_DIM), lambda i: (i, 0)),
        pl.BlockSpec((tm, IDX_DIM), lambda i: (i, 0)),
        pl.BlockSpec((IDX_HEADS, tm), lambda i: (0, i)),
    ]
    return pl.pallas_call(
        _sa_prep_kernel,
        grid=(nt,),
        in_specs=[pl.BlockSpec((tm, 2048), col(C_SQ, 2048)),
                  pl.BlockSpec((tm, 512), col(C_SK, 512)),
                  pl.BlockSpec((tm, 512), col(C_SV, 512)),
                  pl.BlockSpec((tm, 1024), col(C_IQ, 1024)),
                  pl.BlockSpec((tm, 128), lambda i: (i + rb, 0)),
                  pl.BlockSpec((1, SA_DIM), lambda i: (0, 0)),
                  pl.BlockSpec((1, SA_DIM), lambda i: (0, 0))],
        out_specs=out_specs,
        out_shape=out_shape,
        compiler_params=_params(("parallel",)),
        name="sa_prep",
    )(p, p, p, p, small, g_q.reshape(1, SA_DIM), g_k.reshape(1, SA_DIM))


DSA_TQ = 256
BISECT_ITERS = 40
LOG2E = 1.4426950408889634


def _t5_bucket(dist):
    n = jnp.maximum(dist, 0)
    exact = REL_BUCKETS // 2
    log_ratio = jnp.log(jnp.maximum(n, exact).astype(F32) / exact) / math.log(REL_MAX_DIST / exact)
    large = jnp.minimum(exact + (log_ratio * (REL_BUCKETS - exact)).astype(jnp.int32), REL_BUCKETS - 1)
    return jnp.where(n < exact, n, large)


def _topk_threshold(sc_ref, n_tiles, tk, tq):
    inf = jnp.float32(jnp.inf)

    def tile(c):
        return sc_ref[pl.ds(pl.multiple_of(c * tk, tk), tk), :].reshape(tk // 8, 8, tq)

    def mm_body(c, carry):
        mn, mx = carry
        blk = tile(c)
        return (jnp.minimum(mn, jnp.where(blk == -inf, inf, blk).min(axis=0)), jnp.maximum(mx, blk.max(axis=0)))

    mn8, mx8 = lax.fori_loop(0, n_tiles, mm_body, (jnp.full((8, tq), inf, F32), jnp.full((8, tq), -inf, F32)))
    lo = mn8.min(axis=0, keepdims=True)
    hi = mx8.max(axis=0, keepdims=True)

    def count_ge(v):
        def cnt_body(c, acc):
            return acc + jnp.where(tile(c) >= v, 1.0, 0.0).sum(axis=0)
        return lax.fori_loop(0, n_tiles, cnt_body, jnp.zeros((8, tq), F32)).sum(axis=0, keepdims=True)

    def unfinished(cnt_lo):
        return (jnp.max(cnt_lo) > float(TOPK)).astype(jnp.int32)

    def cond(carry):
        it, go = carry[0], carry[1]
        return (it < BISECT_ITERS) & (go > 0)

    def body(carry):
        it, _, lo, hi, cnt_lo = carry
        mid = 0.5 * (lo + hi)
        cnt = count_ge(mid)
        ok = cnt >= float(TOPK)
        lo = jnp.where(ok, mid, lo)
        cnt_lo = jnp.where(ok, cnt, cnt_lo)
        return it + 1, unfinished(cnt_lo), lo, jnp.where(ok, hi, mid), cnt_lo

    cnt_lo = count_ge(lo)
    carry = lax.while_loop(cond, body, (jnp.int32(0), unfinished(cnt_lo), lo, hi, cnt_lo))
    return carry[2]


def _dsa_prompt_kernel(rb_ref, qi_ref, wit_ref, ki_ref, q_ref, k_ref, vt_ref,
                       o_ref, sc_ref, bias_ref, acc_ref, m_ref, l_ref):
    tq = tk = DSA_TQ
    gw = SA_GROUP * tq
    qb = pl.program_id(0)
    ninf = jnp.float32(-jnp.inf)

    @pl.when(qb == 0)
    def _():
        def bias_rows(rc, carry):
            r0 = pl.multiple_of(rc * 8, 8)
            dist = (lax.broadcasted_iota(jnp.int32, (8, tq), 1) + tk) - (r0 + lax.broadcasted_iota(jnp.int32, (8, tq), 0))
            bucket = _t5_bucket(dist)
            hit = [bucket == i for i in range(REL_BUCKETS)]
            for h in range(SA_HEADS):
                val = jnp.zeros((8, tq), F32)
                for i in range(REL_BUCKETS):
                    val = jnp.where(hit[i], rb_ref[i, h], val)
                j = h % SA_GROUP
                bias_ref[h // SA_GROUP, pl.ds(r0, 8), j * tq:(j + 1) * tq] = (val - rb_ref[REL_BUCKETS - 1, h]) * LOG2E
            return carry

        lax.fori_loop(0, 2 * tk // 8, bias_rows, 0)

    def idx_body(kt, carry):
        r0 = pl.multiple_of(kt * tk, tk)
        ki_t = ki_ref[pl.ds(r0, tk), :]
        for h in range(IDX_HEADS):
            part = wit_ref[h:h + 1, :] * jnp.maximum(_dot_nt(ki_t, qi_ref[h]), 0.0)
            if h == 0:
                sc_ref[pl.ds(r0, tk), :] = part
            else:
                sc_ref[pl.ds(r0, tk), :] += part
        return carry

    lax.fori_loop(0, qb + 1, idx_body, 0)
    d0 = pl.multiple_of(qb * tk, tk)
    causal = lax.broadcasted_iota(jnp.int32, (tk, tq), 0) <= lax.broadcasted_iota(jnp.int32, (tk, tq), 1)
    sc_ref[pl.ds(d0, tk), :] = jnp.where(causal, sc_ref[pl.ds(d0, tk), :], ninf)

    thr = _topk_threshold(sc_ref, qb + 1, tk, tq)

    def mask_body(c, carry):
        r0 = pl.multiple_of(c * tk, tk)
        sc_ref[pl.ds(r0, tk), :] = jnp.where(sc_ref[pl.ds(r0, tk), :] >= thr, 0.0, ninf)
        return carry

    lax.fori_loop(0, qb + 1, mask_body, 0)

    m_ref[...] = jnp.full(m_ref.shape, ninf, F32)
    l_ref[...] = jnp.zeros(l_ref.shape, F32)
    acc_ref[...] = jnp.zeros(acc_ref.shape, F32)

    def tile(kt, bias_row0):
        r0 = pl.multiple_of(kt * tk, tk)
        madd1 = sc_ref[pl.ds(r0, tk), :]
        madd = jnp.concatenate([madd1] * SA_GROUP, axis=1)
        for g in range(SA_KV_HEADS):
            qg = q_ref[g * SA_GROUP:(g + 1) * SA_GROUP].reshape(gw, SA_DIM)
            s = _dot_nt(k_ref[g, pl.ds(r0, tk), :], qg) + madd
            if bias_row0 is not None:
                s = s + bias_ref[g, bias_row0:bias_row0 + tk, :]
            m_old = m_ref[g]
            m_new = jnp.maximum(m_old, s.max(axis=0, keepdims=True))
            m_safe = jnp.where(m_new == ninf, 0.0, m_new)
            p = jnp.exp2(s - m_safe)
            alpha = jnp.exp2(m_old - m_safe)
            l_ref[g] = l_ref[g] * alpha + p.sum(axis=0, keepdims=True)
            acc_ref[g] = acc_ref[g] * alpha + _dot(vt_ref[g, kt], p.astype(BF16))
            m_ref[g] = m_new

    def far_body(kt, carry):
        tile(kt, None)
        return carry

    lax.fori_loop(0, jnp.maximum(qb - 1, 0), far_body, 0)

    @pl.when(qb > 0)
    def _():
        tile(qb - 1, 0)

    tile(qb, tk)

    for h in range(SA_HEADS):
        g, j = divmod(h, SA_GROUP)
        out_t = acc_ref[g, :, j * tq:(j + 1) * tq] / l_ref[g, :, j * tq:(j + 1) * tq]
        o_ref[:, h * SA_DIM:(h + 1) * SA_DIM] = out_t.T.astype(BF16)


def _resident(shape):
    nd = len(shape)
    return pl.BlockSpec(shape, lambda i: (0,) * nd, pipeline_mode=pl.Buffered(1))


def _dsa_prompt(qn, kb, vt, qi, kib, wit, rel_bias):
    t = qn.shape[1]
    tq = DSA_TQ
    assert t % tq == 0 and tq >= REL_MAX_DIST
    gw = SA_GROUP * tq
    return pl.pallas_call(
        _dsa_prompt_kernel,
        grid=(t // tq,),
        in_specs=[pl.BlockSpec(memory_space=pltpu.SMEM),
                  pl.BlockSpec((IDX_HEADS, tq, IDX_DIM), lambda i: (0, i, 0)),
                  pl.BlockSpec((IDX_HEADS, tq), lambda i: (0, i)),
                  _resident(kib.shape),
                  pl.BlockSpec((SA_HEADS, tq, SA_DIM), lambda i: (0, i, 0)),
                  _resident(kb.shape),
                  _resident(vt.shape)],
        out_specs=pl.BlockSpec((tq, SA_HEADS * SA_DIM), lambda i: (i, 0)),
        out_shape=jax.ShapeDtypeStruct((t, SA_HEADS * SA_DIM), BF16),
        scratch_shapes=[pltpu.VMEM((t, tq), F32),
                        pltpu.VMEM((SA_KV_HEADS, 2 * tq, gw), F32),
                        pltpu.VMEM((SA_KV_HEADS, SA_DIM, gw), F32),
                        pltpu.VMEM((SA_KV_HEADS, 1, gw), F32),
                        pltpu.VMEM((SA_KV_HEADS, 1, gw), F32)],
        compiler_params=_params(("arbitrary",)),
        name="dsa_prompt",
    )(rel_bias.astype(F32), qi, wit, kib, qn, kb, vt)


def _dn_prep_kernel(x_ref, halo_ref, cw_ref, s_ref, alog_ref, dtb_ref, o_ref, beta_ref, g_ref,
                    *, tm, n_valid, zero_first_halo):
    x = x_ref[...]
    halo = halo_ref[...]
    if zero_first_halo:
        halo = jnp.where(pl.program_id(0) == 0, 0.0, halo)
    xcat = jnp.concatenate([halo, x], axis=0)
    cw = cw_ref[...]
    conv = x * cw[CONV_WIDTH - 1:CONV_WIDTH, :]
    for j in range(CONV_WIDTH - 1):
        lo = 8 - (CONV_WIDTH - 1) + j
        conv = conv + xcat[lo:lo + tm, :] * cw[j:j + 1, :]
    act = conv * jax.nn.sigmoid(conv)
    valid = None
    if n_valid < tm:
        valid = lax.broadcasted_iota(jnp.int32, (tm, 1), 0) < n_valid
    nqk = 2 * DN_HEADS
    for h in range(3 * DN_HEADS):
        a = act[:, h * DN_DIM:(h + 1) * DN_DIM]
        if h < nqk:
            a = a * lax.rsqrt(jnp.sum(a * a, axis=-1, keepdims=True) + EPS)
            if h < DN_HEADS:
                a = a * (DN_DIM ** -0.5)
        if valid is not None:
            a = jnp.where(valid, a, 0.0)
        o_ref[:, h * DN_DIM:(h + 1) * DN_DIM] = a
    s = s_ref[...]
    a_raw = s[:, S_A:S_A + DN_HEADS] + dtb_ref[...]
    softplus = jnp.maximum(a_raw, 0.0) + jnp.log1p(jnp.exp(-jnp.abs(a_raw)))
    g = -jnp.exp(alog_ref[...]) * softplus
    beta = jax.nn.sigmoid(s[:, S_B:S_B + DN_HEADS])
    if valid is not None:
        g = jnp.where(valid, g, 0.0)
        beta = jnp.where(valid, beta, 0.0)
    g_ref[...] = g
    beta_ref[...] = beta


def _dn_prep(x, x_col0, halo, halo_index, small, conv_w, a_log, dt_bias, *, tm, n_valid, zero_first_halo):
    t = small.shape[0]
    c = DN_CONV_CH
    cb = x_col0 // c
    kern = functools.partial(_dn_prep_kernel, tm=tm, n_valid=n_valid, zero_first_halo=zero_first_halo)
    return pl.pallas_call(
        kern,
        grid=(t // tm,),
        in_specs=[pl.BlockSpec((tm, c), lambda i: (i, cb)),
                  pl.BlockSpec((8, c), halo_index),
                  pl.BlockSpec((CONV_WIDTH, c), lambda i: (0, 0)),
                  pl.BlockSpec((tm, 128), lambda i: (i, 0)),
                  pl.BlockSpec((1, DN_HEADS), lambda i: (0, 0)),
                  pl.BlockSpec((1, DN_HEADS), lambda i: (0, 0))],
        out_specs=[pl.BlockSpec((tm, c), lambda i: (i, 0)),
                   pl.BlockSpec((tm, DN_HEADS), lambda i: (i, 0)),
                   pl.BlockSpec((tm, DN_HEADS), lambda i: (i, 0))],
        out_shape=[jax.ShapeDtypeStruct((t, c), F32),
                   jax.ShapeDtypeStruct((t, DN_HEADS), F32),
                   jax.ShapeDtypeStruct((t, DN_HEADS), F32)],
        compiler_params=_params(("parallel",)),
        name="dn_prep",
    )(x, halo, conv_w, small, a_log.reshape(1, DN_HEADS), dt_bias.reshape(1, DN_HEADS))


DN_SOLVE_BLOCK = 16


def _dn_scan_kernel(x_ref, z_ref, beta_ref, g_ref, gt_ref, s0_ref, gout_ref, o_ref, sfin_ref,
                    s_ref, a_ref, t_ref, n_ref, rhs_ref, attn_ref, vnew_ref, *, c):
    n = pl.program_id(1)
    heads = range(DN_HEADS)

    @pl.when(n == 0)
    def _():
        s_ref[...] = s0_ref[0]

    row = lax.broadcasted_iota(jnp.int32, (c, c), 0)
    col = lax.broadcasted_iota(jnp.int32, (c, c), 1)
    incl = row >= col
    strict = row > col
    eye = (row == col).astype(F32)
    sb = min(c, DN_SOLVE_BLOCK)
    same_blk = (row // sb) == (col // sb)
    g_col = g_ref[...]
    g_row = gt_ref[:, 0, 0, :]
    gc_col = jnp.dot(incl.astype(F32), g_col, precision=HI, preferred_element_type=F32)
    gc_row = jnp.dot(g_row, (row <= col).astype(F32), precision=HI, preferred_element_type=F32)
    beta = beta_ref[...]
    gout = gout_ref[...]
    qs = lambda h: x_ref[:, h * DN_DIM:(h + 1) * DN_DIM]
    ks = lambda h: x_ref[:, (DN_HEADS + h) * DN_DIM:(DN_HEADS + h + 1) * DN_DIM]
    vs = lambda h: x_ref[:, (2 * DN_HEADS + h) * DN_DIM:(2 * DN_HEADS + h + 1) * DN_DIM]

    for h in heads:
        q, k, v = qs(h), ks(h), vs(h)
        gcc = gc_col[:, h:h + 1]
        gcr = gc_row[h:h + 1, :]
        bcol = beta[:, h:h + 1]
        decay = jnp.where(incl, jnp.exp(jnp.where(incl, gcc - gcr, 0.0)), 0.0)
        kb = k * bcol
        kt = k.T
        a = jnp.where(strict, -(_dot3(kb, kt) * decay), 0.0)
        a_diag = jnp.where(same_blk, a, 0.0)
        a_ref[h] = a_diag
        t_ref[h] = eye + a_diag
        n_ref[h] = a - a_diag
        rhs_ref[h] = jnp.concatenate([v * bcol, kb * jnp.exp(gcc)], axis=1)
        attn_ref[h] = jnp.where(incl, _dot3(q, kt) * decay, 0.0)

    for _ in range(int(math.log2(sb)) - 1):
        for h in heads:
            p = a_ref[h]
            p2 = _dot3(p, p)
            a_ref[h] = p2
            t = t_ref[h]
            t_ref[h] = t + _dot3(t, p2)

    for i in range(c // sb):
        rows = slice(i * sb, (i + 1) * sb)
        if i > 0:
            for h in heads:
                rhs_ref[h, rows, :] = rhs_ref[h, rows, :] + _dot3(n_ref[h, rows, :], rhs_ref[h])
        for h in heads:
            rhs_ref[h, rows, :] = _dot3(t_ref[h, rows, :], rhs_ref[h])

    for h in heads:
        vnew_ref[h] = rhs_ref[h, :, :DN_DIM] - _dot3(rhs_ref[h, :, DN_DIM:], s_ref[h])

    for h in heads:
        q, k = qs(h), ks(h)
        gcc = gc_col[:, h:h + 1]
        state = s_ref[h]
        v_new = vnew_ref[h]
        out = _dot3(q * jnp.exp(gcc), state) + _dot3(attn_ref[h], v_new)
        g_last = gcc[c - 1:c, :]
        k_dec = k * jnp.exp(g_last - gcc)
        s_ref[h] = state * jnp.exp(g_last) + _dot3(k_dec.T, v_new)
        y = out * lax.rsqrt(jnp.mean(out * out, axis=-1, keepdims=True) + EPS) * gout
        z = z_ref[:, h * DN_DIM:(h + 1) * DN_DIM]
        o_ref[:, h * DN_DIM:(h + 1) * DN_DIM] = (y * (z * jax.nn.sigmoid(z))).astype(o_ref.dtype)

    @pl.when(n == pl.num_programs(1) - 1)
    def _():
        sfin_ref[0] = s_ref[...]


def _dn_scan(xc, z, z_col0, beta, g, s0, g_out, *, bsz, n_chunks, c):
    t = xc.shape[0]
    hdim = DN_HEADS * DN_DIM
    zb = z_col0 // hdim
    gt = g.T.reshape(DN_HEADS, bsz * n_chunks, 1, c)
    kern = functools.partial(_dn_scan_kernel, c=c)
    return pl.pallas_call(
        kern,
        grid=(bsz, n_chunks),
        in_specs=[pl.BlockSpec((c, DN_CONV_CH), lambda b, n: (b * n_chunks + n, 0)),
                  pl.BlockSpec((c, hdim), lambda b, n: (b * n_chunks + n, zb)),
                  pl.BlockSpec((c, DN_HEADS), lambda b, n: (b * n_chunks + n, 0)),
                  pl.BlockSpec((c, DN_HEADS), lambda b, n: (b * n_chunks + n, 0)),
                  pl.BlockSpec((DN_HEADS, 1, 1, c), lambda b, n: (0, b * n_chunks + n, 0, 0)),
                  pl.BlockSpec((1, DN_HEADS, DN_DIM, DN_DIM), lambda b, n: (b, 0, 0, 0)),
                  pl.BlockSpec((1, DN_DIM), lambda b, n: (0, 0))],
        out_specs=[pl.BlockSpec((c, hdim), lambda b, n: (b * n_chunks + n, 0)),
                   pl.BlockSpec((1, DN_HEADS, DN_DIM, DN_DIM), lambda b, n: (b, 0, 0, 0))],
        out_shape=[jax.ShapeDtypeStruct((t, hdim), BF16),
                   jax.ShapeDtypeStruct((bsz, DN_HEADS, DN_DIM, DN_DIM), F32)],
        scratch_shapes=[pltpu.VMEM((DN_HEADS, DN_DIM, DN_DIM), F32),
                        pltpu.VMEM((DN_HEADS, c, c), F32),
                        pltpu.VMEM((DN_HEADS, c, c), F32),
                        pltpu.VMEM((DN_HEADS, c, c), F32),
                        pltpu.VMEM((DN_HEADS, c, 2 * DN_DIM), F32),
                        pltpu.VMEM((DN_HEADS, c, c), F32),
                        pltpu.VMEM((DN_HEADS, c, DN_DIM), F32)],
        compiler_params=_params(("parallel", "arbitrary")),
        name="dn_scan",
    )(xc, z, beta, g, gt, s0, g_out.reshape(1, DN_DIM))


S1_PAGES = 8
S3_PAGES = 4
DEC_T = 4


def _dsa_s_scores_kernel(pt_ref, qi_ref, wsel_ref, kin_ref, *refs):
    pages = refs[:S1_PAGES]
    sc_ref, scn_ref = refs[S1_PAGES:]
    qi = qi_ref[0]
    wsel = wsel_ref[0]

    def score(keys_t):
        r = jnp.maximum(_dot(qi, keys_t.astype(BF16)), 0.0)
        return jnp.dot(wsel, r, precision=HI, preferred_element_type=F32)

    for i in range(S1_PAGES):
        sc_ref[0, :, i * PAGE:(i + 1) * PAGE] = score(pages[i][0])
    sn = score(kin_ref[0])
    t_idx = lax.broadcasted_iota(jnp.int32, sn.shape, 0)
    c_idx = lax.broadcasted_iota(jnp.int32, sn.shape, 1)
    scn_ref[0] = jnp.where((c_idx <= t_idx) & (c_idx < DEC_T), sn, -jnp.inf)


def _dsa_s_scores(page_table, qi_s, wsel, ki_new_t, cache_kidx_t):
    dbsz, n_pages = page_table.shape
    past = n_pages * PAGE
    steps = n_pages // S1_PAGES
    page_spec = lambda i: pl.BlockSpec((1, IDX_DIM, PAGE), lambda b, j, pt: (pt[b * n_pages + j * S1_PAGES + i], 0, 0))
    grid_spec = pltpu.PrefetchScalarGridSpec(
        num_scalar_prefetch=1,
        grid=(dbsz, steps),
        in_specs=[pl.BlockSpec((1,) + qi_s.shape[1:], lambda b, j, pt: (b, 0, 0)),
                  pl.BlockSpec((1,) + wsel.shape[1:], lambda b, j, pt: (b, 0, 0)),
                  pl.BlockSpec((1, IDX_DIM, PAGE), lambda b, j, pt: (b, 0, 0))]
                 + [page_spec(i) for i in range(S1_PAGES)],
        out_specs=[pl.BlockSpec((1, 8, S1_PAGES * PAGE), lambda b, j, pt: (b, 0, j)),
                   pl.BlockSpec((1, 8, PAGE), lambda b, j, pt: (b, 0, 0))],
    )
    return pl.pallas_call(
        _dsa_s_scores_kernel,
        grid_spec=grid_spec,
        out_shape=[jax.ShapeDtypeStruct((dbsz, 8, past), F32), jax.ShapeDtypeStruct((dbsz, 8, PAGE), F32)],
        compiler_params=_params(("parallel", "arbitrary")),
        name="dsa_sample_scores",
    )(page_table.reshape(-1), qi_s, wsel, ki_new_t, *([cache_kidx_t] * S1_PAGES))


def _dsa_s_mask_kernel(sc_ref, scn_ref, m_ref, mn_ref):
    inf = jnp.float32(jnp.inf)

    def row_min(s):
        return jnp.where(s == -inf, inf, s).min(axis=1, keepdims=True)

    lo = jnp.minimum(row_min(sc_ref[...]), row_min(scn_ref[...]))
    hi = jnp.maximum(sc_ref[...].max(axis=1, keepdims=True), scn_ref[...].max(axis=1, keepdims=True))

    def count(s, mid):
        return jnp.where(s >= mid, 1.0, 0.0).sum(axis=1, keepdims=True)

    def bis_body(_, carry):
        lo, hi = carry
        mid = 0.5 * (lo + hi)
        ok = (count(sc_ref[...], mid) + count(scn_ref[...], mid)) >= float(TOPK)
        return jnp.where(ok, mid, lo), jnp.where(ok, hi, mid)

    thr, _ = lax.fori_loop(0, BISECT_ITERS, bis_body, (lo, hi))
    m_ref[...] = jnp.where(sc_ref[...] >= thr, 0.0, -inf)
    mn_ref[...] = jnp.where(scn_ref[...] >= thr, 0.0, -inf)


def _dsa_s_mask(sc, scn):
    rows, past = sc.shape
    return pl.pallas_call(
        _dsa_s_mask_kernel,
        out_shape=[jax.ShapeDtypeStruct((rows, past), F32), jax.ShapeDtypeStruct((rows, PAGE), F32)],
        compiler_params=_params(None),
        name="dsa_sample_mask",
    )(sc, scn)


def _dsa_s_attn_kernel(pt_ref, q_ref, m_ref, mn_ref, kn_ref, vn_ref, hm_ref, bias_ref, *refs, n_steps):
    kp = refs[:S3_PAGES]
    vp = refs[S3_PAGES:2 * S3_PAGES]
    o_ref, acc_ref, mx_ref, l_ref = refs[2 * S3_PAGES:]
    j = pl.program_id(1)
    ninf = jnp.float32(-jnp.inf)
    nrow = SA_HEADS * DEC_T
    pcols = PAGE * SA_KV_HEADS
    q = q_ref[0]

    def update(s, v_bf16, first):
        m_old = jnp.full((nrow, 1), ninf, F32) if first else mx_ref[:, 0:1]
        m_new = jnp.maximum(m_old, s.max(axis=1, keepdims=True))
        m_safe = jnp.where(m_new == ninf, 0.0, m_new)
        p = jnp.exp2(s - m_safe)
        pv = _dot(p.astype(BF16), v_bf16)
        if first:
            l_new = p.sum(axis=1, keepdims=True)
            acc_new = pv
        else:
            alpha = jnp.exp2(m_old - m_safe)
            l_new = l_ref[:, 0:1] * alpha + p.sum(axis=1, keepdims=True)
            acc_new = acc_ref[...] * alpha + pv
        mx_ref[...] = jnp.broadcast_to(m_new, (nrow, 128))
        l_ref[...] = jnp.broadcast_to(l_new, (nrow, 128))
        acc_ref[...] = acc_new

    def expand(mask_t):
        return jnp.concatenate([mask_t[0:DEC_T]] * SA_HEADS, axis=0)

    hm = hm_ref[...]

    @pl.when(j == 0)
    def _():
        s = _dot_nt(q, kn_ref[0]) + bias_ref[:, pcols:pcols + PAGE] + hm[:, :PAGE] + expand(mn_ref[0])
        update(s, vn_ref[0], True)

    is_last = j == n_steps - 1
    k = jnp.concatenate([kp[i][0] for i in range(S3_PAGES)], axis=0).astype(BF16)
    v = jnp.concatenate([vp[i][0] for i in range(S3_PAGES)], axis=0).astype(BF16)
    tail = jnp.where(is_last, bias_ref[:, 0:pcols], 0.0)
    bias = jnp.concatenate([hm] * (S3_PAGES - 1) + [hm + tail], axis=1)
    s = _dot_nt(q, k) + bias + expand(m_ref[0])
    update(s, v, False)

    @pl.when(is_last)
    def _():
        o_ref[0] = acc_ref[...] / l_ref[...]


def _dsa_s_attn(page_table, q_s, madd_x, madd_new_x, k_new, v_new, head_mask, bias, cache_k, cache_v):
    dbsz, n_pages = page_table.shape
    n_steps = n_pages // S3_PAGES
    nrow = SA_HEADS * DEC_T
    pcols = PAGE * SA_KV_HEADS
    page_spec = lambda i: pl.BlockSpec((1, pcols, SA_DIM), lambda b, j, pt: (pt[b * n_pages + j * S3_PAGES + i], 0, 0))
    grid_spec = pltpu.PrefetchScalarGridSpec(
        num_scalar_prefetch=1,
        grid=(dbsz, n_steps),
        in_specs=[pl.BlockSpec((1, nrow, SA_DIM), lambda b, j, pt: (b, 0, 0)),
                  pl.BlockSpec((1, 8, S3_PAGES * pcols), lambda b, j, pt: (b, 0, j)),
                  pl.BlockSpec((1, 8, PAGE), lambda b, j, pt: (b, 0, 0)),
                  pl.BlockSpec((1, PAGE, SA_DIM), lambda b, j, pt: (b, 0, 0)),
                  pl.BlockSpec((1, PAGE, SA_DIM), lambda b, j, pt: (b, 0, 0)),
                  pl.BlockSpec((nrow, pcols), lambda b, j, pt: (0, 0)),
                  pl.BlockSpec((nrow, pcols + PAGE), lambda b, j, pt: (0, 0))]
                 + [page_spec(i) for i in range(S3_PAGES)] * 2,
        out_specs=pl.BlockSpec((1, nrow, SA_DIM), lambda b, j, pt: (b, 0, 0)),
        scratch_shapes=[pltpu.VMEM((nrow, SA_DIM), F32), pltpu.VMEM((nrow, 128), F32), pltpu.VMEM((nrow, 128), F32)],
    )
    return pl.pallas_call(
        functools.partial(_dsa_s_attn_kernel, n_steps=n_steps),
        grid_spec=grid_spec,
        out_shape=jax.ShapeDtypeStruct((dbsz, nrow, SA_DIM), F32),
        compiler_params=_params(("parallel", "arbitrary")),
        name="dsa_sample_attn",
    )(page_table.reshape(-1), q_s, madd_x, madd_new_x, k_new, v_new, head_mask, bias,
      *([cache_k] * S3_PAGES), *([cache_v] * S3_PAGES))


def _dsa_sample(qn, ko, vo, qi, kio, wit, cache_k, cache_v, cache_kidx_t, page_table, rel_bias):
    dbsz, n_pages = page_table.shape
    t = DEC_T
    nkv = SA_KV_HEADS
    past = n_pages * PAGE
    assert n_pages % S1_PAGES == 0 and n_pages % S3_PAGES == 0 and PAGE >= REL_MAX_DIST and PAGE >= nkv * t
    n_pool = cache_k.shape[0]
    qi_s = qi.reshape(IDX_HEADS, dbsz, t, IDX_DIM).transpose(1, 2, 0, 3).reshape(dbsz, t * IDX_HEADS, IDX_DIM)
    w_bth = wit.reshape(IDX_HEADS, dbsz, t).transpose(1, 2, 0)
    wsel = (w_bth[:, :, None, :] * jnp.eye(t, dtype=F32)[None, :, :, None]).reshape(dbsz, t, t * IDX_HEADS)
    wsel = jnp.pad(wsel, ((0, 0), (0, 8 - t), (0, 0)))
    ki_new_t = jnp.pad(kio.reshape(dbsz, t, IDX_DIM).transpose(0, 2, 1), ((0, 0), (0, 0), (0, PAGE - t)))
    sc, scn = _dsa_s_scores(page_table, qi_s, wsel, ki_new_t, cache_kidx_t)
    madd, madd_new = _dsa_s_mask(sc.reshape(dbsz * 8, past), scn.reshape(dbsz * 8, PAGE))
    madd_x = jnp.repeat(madd.reshape(dbsz, 8, past), nkv, axis=-1)
    madd_new_x = jnp.repeat(madd_new.reshape(dbsz, 8, PAGE)[:, :, :PAGE // nkv], nkv, axis=-1)
    new_rows = lambda a: jnp.pad(a.reshape(dbsz, t * nkv, SA_DIM), ((0, 0), (0, PAGE - t * nkv), (0, 0))).astype(BF16)
    q_s = qn.reshape(nkv, SA_GROUP, dbsz, t, SA_DIM).transpose(2, 0, 1, 3, 4).reshape(dbsz, SA_HEADS * t, SA_DIM)
    nrow = SA_HEADS * t
    r = jnp.arange(nrow, dtype=jnp.int32)
    c = jnp.arange(PAGE * nkv, dtype=jnp.int32)
    head_mask = jnp.where((c[None, :] % nkv) == (r[:, None] // (SA_GROUP * t)), 0.0, -jnp.inf).astype(F32)
    dist = jnp.concatenate([PAGE + (r % t)[:, None] - (c // nkv)[None, :],
                            (r % t)[:, None] - (c[:PAGE] // nkv)[None, :]], axis=1)
    rb_rows = jnp.repeat(rel_bias.astype(F32).T, t, axis=0)
    onehot = _t5_bucket(dist)[:, :, None] == jnp.arange(REL_BUCKETS, dtype=jnp.int32)
    bias = (jnp.sum(jnp.where(onehot, rb_rows[:, None, :], 0.0), axis=-1) - rb_rows[:, REL_BUCKETS - 1:]) * LOG2E
    o = _dsa_s_attn(page_table, q_s, madd_x, madd_new_x, new_rows(ko), new_rows(vo), head_mask, bias,
                    cache_k.reshape(n_pool, PAGE * nkv, SA_DIM), cache_v.reshape(n_pool, PAGE * nkv, SA_DIM))
    return o.reshape(dbsz, nkv, SA_GROUP, t, SA_DIM).transpose(0, 3, 1, 2, 4).reshape(dbsz * t, SA_HEADS * SA_DIM)


def _merge_kernel(odn_ref, osa_ref, gd_ref, gs_ref, w1_ref, w2_ref, o_ref):
    a = _dot(odn_ref[...], w1_ref[...])
    b = _dot(osa_ref[...], w2_ref[...])
    o_ref[...] = (jax.nn.sigmoid(gd_ref[...]) * a + jax.nn.sigmoid(gs_ref[...]) * b).astype(o_ref.dtype)


def _merge(o_dn, o_sa, p, w1, w2, tn=512):
    m, k = o_dn.shape
    n = w1.shape[1]
    tm = _row_tile(m)
    return pl.pallas_call(
        _merge_kernel,
        grid=(n // tn, m // tm),
        in_specs=[pl.BlockSpec((tm, k), lambda j, i: (i, 0)),
                  pl.BlockSpec((tm, k), lambda j, i: (i, 0)),
                  pl.BlockSpec((tm, tn), lambda j, i: (i, C_GD // tn + j)),
                  pl.BlockSpec((tm, tn), lambda j, i: (i, C_GS // tn + j)),
                  pl.BlockSpec((k, tn), lambda j, i: (0, j)),
                  pl.BlockSpec((k, tn), lambda j, i: (0, j))],
        out_specs=pl.BlockSpec((tm, tn), lambda j, i: (i, j)),
        out_shape=jax.ShapeDtypeStruct((m, n), BF16),
        compiler_params=_params(("parallel", "parallel")),
        name="merge",
    )(o_dn, o_sa, p, p, w1, w2)


def _pack_bf16_pair(lo, hi):
    lo_bits = pltpu.bitcast(lo.astype(BF16).astype(F32), jnp.uint32)
    hi_bits = pltpu.bitcast(hi.astype(BF16).astype(F32), jnp.uint32)
    return (hi_bits & jnp.uint32(0xFFFF0000)) | (lo_bits >> 16)


def _unpack_bf16_pair(w):
    lo = pltpu.bitcast(w << 16, F32)
    hi = pltpu.bitcast(w & jnp.uint32(0xFFFF0000), F32)
    return jnp.concatenate([lo, hi], axis=1).astype(BF16)


def _post_attn_kernel(xp_ref, xs_ref, m_ref, g1_ref, sh_ref, sc_ref, gn_ref, wr_ref, br_ref, x1_ref, hp_ref, lg_ref,
                      *, n_prompt_tiles):
    x = jnp.where(pl.program_id(0) < n_prompt_tiles, xp_ref[...], xs_ref[...])
    x1 = x + g1_ref[...] * m_ref[...]
    x1_ref[...] = x1
    y = x1 * lax.rsqrt(jnp.mean(x1 * x1, axis=-1, keepdims=True) + EPS) * gn_ref[...]
    h2 = y * (1.0 + sc_ref[...]) + sh_ref[...]
    half = h2.shape[1] // 2
    hp_ref[...] = _pack_bf16_pair(h2[:, :half], h2[:, half:])
    lg_ref[...] = lax.dot_general(wr_ref[...], h2, (((1,), (1,)), ((), ())), precision=HI,
                                  preferred_element_type=F32) + br_ref[...]


def _post_attn(x_p, x_s, m2, mod_exp, g_norm2, w_router_t, b_router, n_prompt_tiles):
    t, d = m2.shape
    rb = _mod_row_block(n_prompt_tiles)
    tile = lambda c: pl.BlockSpec((TOK_TILE, d), lambda i: (rb(i), c))
    return pl.pallas_call(
        functools.partial(_post_attn_kernel, n_prompt_tiles=n_prompt_tiles),
        grid=(t // TOK_TILE,),
        in_specs=_token_specs(d, n_prompt_tiles) + [
            pl.BlockSpec((TOK_TILE, d), lambda i: (i, 0)),
            tile(2), tile(3), tile(4),
            pl.BlockSpec((1, d), lambda i: (0, 0)),
            pl.BlockSpec((N_EXPERTS, d), lambda i: (0, 0)),
            pl.BlockSpec((N_EXPERTS, 1), lambda i: (0, 0))],
        out_specs=[pl.BlockSpec((TOK_TILE, d), lambda i: (i, 0)),
                   pl.BlockSpec((TOK_TILE, d // 2), lambda i: (i, 0)),
                   pl.BlockSpec((N_EXPERTS, TOK_TILE), lambda i: (0, i))],
        out_shape=[jax.ShapeDtypeStruct((t, d), F32),
                   jax.ShapeDtypeStruct((t, d // 2), jnp.uint32),
                   jax.ShapeDtypeStruct((N_EXPERTS, t), F32)],
        compiler_params=_params(("parallel",)),
        name="post_attn",
    )(x_p, x_s, m2, mod_exp, mod_exp, mod_exp, g_norm2.reshape(1, d), w_router_t, b_router.reshape(N_EXPERTS, 1))


def _route_kernel(lg_ref, e_ref, w_ref, r_ref, cnt_ref, run_ref):
    @pl.when(pl.program_id(0) == 0)
    def _():
        run_ref[...] = jnp.zeros(run_ref.shape, F32)

    tt = TOK_TILE
    lg = lg_ref[...]
    e_iota = lax.broadcasted_iota(jnp.int32, lg.shape, 0)
    sels, tops = [], []
    for r in range(TOP_K):
        m = lg.max(axis=0, keepdims=True)
        e = jnp.where(lg == m, e_iota, N_EXPERTS).min(axis=0, keepdims=True)
        sel = e_iota == e
        lg = jnp.where(sel, -jnp.inf, lg)
        e_ref[r:r + 1, :] = e
        sels.append(sel)
        tops.append(m)
    ex = [jnp.exp(m - tops[0]) for m in tops]
    denom = ex[0] + ex[1] + ex[2] + ex[3]
    for r in range(TOP_K):
        w_ref[r:r + 1, :] = ex[r] / denom
    onehot = sels[0] | sels[1] | sels[2] | sels[3]
    oh = jnp.where(onehot, 1.0, 0.0).astype(BF16)
    before = (lax.broadcasted_iota(jnp.int32, (tt, tt), 0) < lax.broadcasted_iota(jnp.int32, (tt, tt), 1))
    prefix = _dot(oh, jnp.where(before, 1.0, 0.0).astype(BF16))
    total = _dot(oh, jnp.ones((tt, tt), BF16))
    base = run_ref[...] + prefix
    for r in range(TOP_K):
        r_ref[r:r + 1, :] = jnp.where(sels[r], base, 0.0).sum(axis=0, keepdims=True).astype(jnp.int32)
    run_ref[...] = run_ref[...] + total
    cnt_ref[...] = run_ref[...]


def _route(logits_t):
    e, t = logits_t.shape
    spec4 = pl.BlockSpec((TOP_K, TOK_TILE), lambda i: (0, i))
    return pl.pallas_call(
        _route_kernel,
        grid=(t // TOK_TILE,),
        in_specs=[pl.BlockSpec((e, TOK_TILE), lambda i: (0, i))],
        out_specs=[spec4, spec4, spec4, pl.BlockSpec((e, TOK_TILE), lambda i: (0, 0))],
        out_shape=[jax.ShapeDtypeStruct((TOP_K, t), jnp.int32),
                   jax.ShapeDtypeStruct((TOP_K, t), F32),
                   jax.ShapeDtypeStruct((TOP_K, t), jnp.int32),
                   jax.ShapeDtypeStruct((e, TOK_TILE), F32)],
        scratch_shapes=[pltpu.VMEM((e, TOK_TILE), F32)],
        compiler_params=_params(("arbitrary",)),
        name="route",
    )(logits_t)


def _dispatch_kernel(idx_ref, h_ref, o_ref):
    def body(r, carry):
        o_ref[pl.ds(r, 1), :] = h_ref[pl.ds(idx_ref[r], 1), :]
        return carry

    lax.fori_loop(0, MOE_BLOCK, body, 0, unroll=8)


def _dispatch(row_tok, h2p):
    rows = row_tok.shape[0]
    t, w = h2p.shape
    return pl.pallas_call(
        _dispatch_kernel,
        grid=(rows // MOE_BLOCK,),
        in_specs=[pl.BlockSpec((MOE_BLOCK,), lambda i: (i,), memory_space=pltpu.SMEM),
                  pl.BlockSpec((t, w), lambda i: (0, 0), pipeline_mode=pl.Buffered(1))],
        out_specs=pl.BlockSpec((MOE_BLOCK, w), lambda i: (i, 0)),
        out_shape=jax.ShapeDtypeStruct((rows, w), jnp.uint32),
        compiler_params=_params(("parallel",)),
        name="moe_dispatch",
    )(row_tok, h2p)


MOE_GROUP = 4
FF_TILE = 512
N_FF = D_FF // FF_TILE
ST_BLK, ST_F, ST_E, ST_J, ST_CAST, ST_VALID, ST_DONE = range(7)


def _expert_kernel(st_ref, dest_ref, x_ref, wg_ref, wu_ref, wd_ref, bg_ref, bu_ref, bd_ref, ys_ref,
                   wgb_ref, wub_ref, wdb_ref, acc_ref, stage_ref, sem_ref, inflight_ref):
    s = pl.program_id(0)
    f = st_ref[ST_F, s]
    j = st_ref[ST_J, s]

    def row_copy(slot, r, dst_row):
        return pltpu.make_async_copy(stage_ref.at[slot, pl.ds(r, 1)], ys_ref.at[pl.ds(dst_row, 1)], sem_ref.at[slot])

    def drain(slot):
        def body(r, carry):
            row_copy(slot, 0, 0).wait()
            return carry
        lax.fori_loop(0, inflight_ref[slot], body, 0)
        inflight_ref[slot] = 0

    @pl.when(s == 0)
    def _():
        inflight_ref[0] = 0
        inflight_ref[1] = 0

    @pl.when(st_ref[ST_CAST, s] == 1)
    def _():
        wgb_ref[...] = wg_ref[0].astype(BF16)
        wub_ref[...] = wu_ref[0].astype(BF16)
        wdb_ref[...] = wd_ref[0].astype(BF16)

    @pl.when(st_ref[ST_VALID, s] == 1)
    def _():
        x = _unpack_bf16_pair(x_ref[...])
        gate = jnp.minimum(_dot(x, wgb_ref[...]) + bg_ref[0], SWIGLU_LIMIT)
        up = jnp.clip(_dot(x, wub_ref[...]) + bu_ref[0], -SWIGLU_LIMIT, SWIGLU_LIMIT)
        act = (up + 1.0) * gate * jax.nn.sigmoid(SWIGLU_ALPHA * gate)
        part = _dot(act.astype(BF16), wdb_ref[...])

        @pl.when(f == 0)
        def _():
            acc_ref[j] = part

        @pl.when(f > 0)
        def _():
            acc_ref[j] = acc_ref[j] + part

        @pl.when(f == N_FF - 1)
        def _():
            slot = st_ref[ST_DONE, s] % 2
            drain(slot)
            stage_ref[slot] = acc_ref[j] + bd_ref[0]

            def body(r, n):
                dst = dest_ref[r]

                @pl.when(dst >= 0)
                def _():
                    row_copy(slot, r, dst).start()
                return n + jnp.where(dst >= 0, 1, 0)
            inflight_ref[slot] = lax.fori_loop(0, MOE_BLOCK, body, 0)

    @pl.when(s == pl.num_programs(0) - 1)
    def _():
        drain(0)
        drain(1)


def _experts(steps, dest_row, xg, w_up, b_up, w_down, b_down, n_out_rows):
    n_steps = steps.shape[1]
    rows, wpk = xg.shape
    d = w_down.shape[2]
    e = w_up.shape[0]
    grid_spec = pltpu.PrefetchScalarGridSpec(
        num_scalar_prefetch=1,
        grid=(n_steps,),
        in_specs=[pl.BlockSpec((MOE_BLOCK,), lambda s, st: (st[ST_BLK, s],), memory_space=pltpu.SMEM),
                  pl.BlockSpec((MOE_BLOCK, wpk), lambda s, st: (st[ST_BLK, s], 0)),
                  pl.BlockSpec((1, d, FF_TILE), lambda s, st: (st[ST_E, s], 0, st[ST_F, s])),
                  pl.BlockSpec((1, d, FF_TILE), lambda s, st: (st[ST_E, s], 0, N_FF + st[ST_F, s])),
                  pl.BlockSpec((1, FF_TILE, d), lambda s, st: (st[ST_E, s], st[ST_F, s], 0)),
                  pl.BlockSpec((1, 1, FF_TILE), lambda s, st: (st[ST_E, s], 0, st[ST_F, s])),
                  pl.BlockSpec((1, 1, FF_TILE), lambda s, st: (st[ST_E, s], 0, N_FF + st[ST_F, s])),
                  pl.BlockSpec((1, 1, d), lambda s, st: (st[ST_E, s], 0, 0))],
        out_specs=pl.BlockSpec(memory_space=pl.ANY),
        scratch_shapes=[pltpu.VMEM((d, FF_TILE), BF16), pltpu.VMEM((d, FF_TILE), BF16), pltpu.VMEM((FF_TILE, d), BF16),
                        pltpu.VMEM((MOE_GROUP, MOE_BLOCK, d), F32),
                        pltpu.VMEM((2, MOE_BLOCK, d), F32),
                        pltpu.SemaphoreType.DMA((2,)),
                        pltpu.SMEM((2,), jnp.int32)],
    )
    return pl.pallas_call(
        _expert_kernel,
        grid_spec=grid_spec,
        out_shape=jax.ShapeDtypeStruct((n_out_rows, d), F32),
        compiler_params=_params(("arbitrary",)),
        name="moe_experts",
    )(steps, dest_row, xg, w_up, w_up, w_down, b_up.reshape(e, 1, -1), b_up.reshape(e, 1, -1), b_down.reshape(e, 1, d))


def _moe_plan(top_e, rank, counts, t):
    blk = MOE_BLOCK
    n_assign = TOP_K * t
    n_blocks = -(-n_assign // blk) + N_EXPERTS
    rows = n_blocks * blk
    nblk_e = (counts + blk - 1) // blk
    blk_end = jnp.cumsum(nblk_e)
    blk_start = blk_end - nblk_e
    used = blk_end[-1]
    e_ids = jnp.arange(N_EXPERTS, dtype=jnp.int32)
    start_of = jnp.sum(jnp.where(top_e[:, :, None] == e_ids, blk_start, 0), axis=-1)
    dest = start_of * blk + rank
    flat_slot_tok = (jnp.arange(TOP_K, dtype=jnp.int32)[:, None] * t + jnp.arange(t, dtype=jnp.int32)[None, :])
    row_tok = jnp.zeros((rows,), jnp.int32).at[dest.reshape(-1)].set(jnp.tile(jnp.arange(t, dtype=jnp.int32), TOP_K))
    dest_row = jnp.full((rows,), -1, jnp.int32).at[dest.reshape(-1)].set(flat_slot_tok.reshape(-1))
    b = jnp.arange(n_blocks, dtype=jnp.int32)
    e_b = jnp.minimum(jnp.sum(blk_end[None, :] <= b[:, None], axis=1), N_EXPERTS - 1).astype(jnp.int32)
    lb = b - blk_start[e_b]
    j_b = lb % MOE_GROUP
    gsize = jnp.minimum(MOE_GROUP, nblk_e[e_b] - (lb - j_b))
    valid_b = b < used
    fidx = jnp.arange(N_FF, dtype=jnp.int32)
    step_of = N_FF * (b - j_b)[:, None] + fidx[None, :] * gsize[:, None] + j_b[:, None]
    n_steps = N_FF * n_blocks
    step_of = jnp.where(valid_b[:, None], step_of, n_steps)
    def scat(vals, fill):
        return jnp.full((n_steps,), fill, jnp.int32).at[step_of.reshape(-1)].set(
            jnp.broadcast_to(vals, (n_blocks, N_FF)).reshape(-1).astype(jnp.int32), mode="drop")
    st_valid = scat(jnp.ones((n_blocks, 1), jnp.int32), 0)
    st_blk = scat(b[:, None], -1)
    st_f = scat(fidx[None, :], -1)
    st_e = scat(e_b[:, None], -1)
    st_j = scat(j_b[:, None], 0)
    st_cast = scat((j_b == 0)[:, None], 0)
    n_valid_steps = N_FF * used
    last_idx = jnp.maximum(n_valid_steps - 1, 0)
    pad = jnp.arange(n_steps) >= n_valid_steps
    fix = lambda a: jnp.where(pad, a[last_idx], a)
    st_blk, st_f, st_e = fix(st_blk), fix(st_f), fix(st_e)
    finishing = (st_valid == 1) & (st_f == N_FF - 1)
    st_done = jnp.cumsum(finishing.astype(jnp.int32)) - finishing.astype(jnp.int32)
    steps = jnp.stack([st_blk, st_f, st_e, st_j, st_cast, st_valid, st_done]).astype(jnp.int32)
    return steps, row_tok, dest_row, rows


def _final_kernel(x1_ref, g2_ref, w_ref, y0_ref, y1_ref, y2_ref, y3_ref, op_ref, os_ref, *, n_prompt_tiles):
    w = w_ref[...]
    moe = (w[:, 0:1] * y0_ref[...] + w[:, 1:2] * y1_ref[...]) + (w[:, 2:3] * y2_ref[...] + w[:, 3:4] * y3_ref[...])
    y = x1_ref[...] + g2_ref[...] * moe
    i = pl.program_id(0)

    @pl.when(i < n_prompt_tiles)
    def _():
        op_ref[...] = y

    @pl.when(i >= n_prompt_tiles)
    def _():
        os_ref[...] = y


def _final(x1, mod_exp, top_w_t, ys, n_prompt_tiles):
    t, d = x1.shape
    rb = _mod_row_block(n_prompt_tiles)
    nt = t // TOK_TILE
    slot = lambda k: pl.BlockSpec((TOK_TILE, d), lambda i: (k * nt + i, 0))
    return pl.pallas_call(
        functools.partial(_final_kernel, n_prompt_tiles=n_prompt_tiles),
        grid=(nt,),
        in_specs=[pl.BlockSpec((TOK_TILE, d), lambda i: (i, 0)),
                  pl.BlockSpec((TOK_TILE, d), lambda i: (rb(i), 5)),
                  pl.BlockSpec((TOK_TILE, TOP_K), lambda i: (i, 0)),
                  slot(0), slot(1), slot(2), slot(3)],
        out_specs=_token_specs(d, n_prompt_tiles),
        out_shape=[jax.ShapeDtypeStruct((n_prompt_tiles * TOK_TILE, d), F32),
                   jax.ShapeDtypeStruct(((nt - n_prompt_tiles) * TOK_TILE, d), F32)],
        compiler_params=_params(("arbitrary",)),
        name="moe_combine",
    )(x1, mod_exp, top_w_t, ys, ys, ys, ys)


def _moe(h2p, logits_t, w_up, b_up, w_down, b_down):
    t = h2p.shape[0]
    top_e, top_w, rank, cnt = _route(logits_t)
    counts = cnt[:, 0].astype(jnp.int32)
    steps, row_tok, dest_row, rows = _moe_plan(top_e, rank, counts, t)
    xg = _dispatch(row_tok, h2p)
    ys = _experts(steps, dest_row, xg, w_up, b_up, w_down, b_down, TOP_K * t)
    return ys, top_w.T


PROJ_SIZES = (DN_CONV_CH, DN_HEADS * DN_DIM, DN_HEADS, DN_HEADS, SA_HEADS * SA_DIM, SA_KV_HEADS * SA_DIM,
              SA_KV_HEADS * SA_DIM, IDX_HEADS * IDX_DIM, IDX_DIM, IDX_HEADS, D_MODEL, D_MODEL)


def _split_w_in(w_in_t):
    ends = np.cumsum(PROJ_SIZES)
    seg = [w_in_t[int(e - s):int(e)] for s, e in zip(PROJ_SIZES, ends)]
    (dn_qkv, dn_z, dn_a, dn_b, sa_q, sa_k, sa_v, ix_q, ix_k, ix_w, gate_dn, gate_sa) = seg
    w_big = jnp.concatenate([dn_qkv, dn_z, sa_q, sa_k, sa_v, ix_q, gate_dn, gate_sa], axis=0).astype(BF16)
    pad = jnp.zeros((128 - (IDX_DIM + 2 * DN_HEADS + IDX_HEADS), w_in_t.shape[1]), w_in_t.dtype)
    w_small = jnp.concatenate([ix_k, dn_a, dn_b, ix_w, pad], axis=0)
    return w_big, w_small


def _pad_seq(a, bsz, t, t_pad, front=0):
    a = a.reshape(bsz, t, -1)
    return jnp.pad(a, ((0, 0), (front, t_pad - t - front), (0, 0))).reshape(bsz * t_pad, -1)


def kernel(x_prompt, x_sample, cache_k, cache_v, cache_kidx, state_conv, state_ssm, page_table, c_prompt, c_sample,
           rel_bias, w_ada, b_ada, g_norm1, w_in, conv_w, a_log, dt_bias, g_dn_out, g_q, g_k, w_o_dn, w_o_sa, w_out,
           g_norm2, w_router, b_router, w_up, b_up, w_down, b_down):
    assert w_ada.shape[0] == 1 and x_prompt.shape[0] == 1
    d = D_MODEL
    tp = x_prompt.shape[1]
    dbsz, dt = x_sample.shape[:2]
    ts = dbsz * dt
    assert dt == DEC_T and ts == TOK_TILE and tp % DSA_TQ == 0
    npt = tp // TOK_TILE
    x_p = x_prompt.reshape(tp, d)
    x_s = x_sample.reshape(ts, d)

    c_all = jnp.concatenate([c_prompt, c_sample], axis=0)
    c_all = jnp.pad(c_all, ((0, -c_all.shape[0] % 8), (0, 0)))
    mod = _adaln(c_all, w_ada[0], b_ada[0])
    mod_exp = jnp.concatenate([jnp.broadcast_to(mod[0:1], (TOK_TILE, mod.shape[1])),
                               jnp.repeat(mod[1:1 + dbsz], dt, axis=0)], axis=0)

    w_big_t, w_small_t = _split_w_in(jnp.swapaxes(w_in, 1, 2)[0])
    h, small = _norm_mod(x_p, x_s, g_norm1[0], mod_exp, w_small_t, npt)
    p = _matmul(h, w_big_t, F32, w_transposed=True)

    qn, ko_p, kb, vo_p, vt, qi, kio_p, kib, wit = _sa_prep(p, small, g_q[0], g_k[0], 0, tp, DSA_TQ)
    o_sa_p = _dsa_prompt(qn, kb, vt, qi, kib, wit, rel_bias)
    qn_s, ko_s, _, vo_s, _, qi_s, kio_s, _, wit_s = _sa_prep(p, small, g_q[0], g_k[0], tp, ts, TOK_TILE)
    n_pool = cache_k.shape[1]
    o_sa_s = _dsa_sample(qn_s, ko_s, vo_s, qi_s, kio_s, wit_s, cache_k.reshape(n_pool, *cache_k.shape[2:]),
                         cache_v.reshape(n_pool, *cache_v.shape[2:]),
                         jnp.swapaxes(cache_kidx, 2, 3).reshape(n_pool, IDX_DIM, PAGE), page_table, rel_bias)

    tm = 256
    xc, beta, g = _dn_prep(p, C_QKV, p, lambda i: (jnp.maximum(i * (tm // 8) - 1, 0), 0), small[:tp],
                           conv_w[0], a_log[0], dt_bias[0], tm=tm, n_valid=tm, zero_first_halo=True)
    chunk = 64
    s0_p = jnp.zeros((1,) + state_ssm.shape[2:], F32)
    o_dn_p, ssm_p = _dn_scan(xc, p, C_Z, beta, g, s0_p, g_dn_out[0], bsz=1, n_chunks=tp // chunk, c=chunk)
    qkv_s = p[tp:, C_QKV:C_QKV + DN_CONV_CH]
    halo_s = _pad_seq(state_conv[0].reshape(dbsz * (CONV_WIDTH - 1), -1), dbsz, CONV_WIDTH - 1, 8, front=8 - (CONV_WIDTH - 1))
    xc_s, beta_s, g_s = _dn_prep(_pad_seq(qkv_s, dbsz, dt, 8), 0, halo_s, lambda i: (i, 0), _pad_seq(small[tp:], dbsz, dt, 8),
                                 conv_w[0], a_log[0], dt_bias[0], tm=8, n_valid=dt, zero_first_halo=False)
    z_s = _pad_seq(p[tp:, C_Z:C_Z + DN_HEADS * DN_DIM], dbsz, dt, 8)
    o_dn_s, ssm_s = _dn_scan(xc_s, z_s, 0, beta_s, g_s, state_ssm[0], g_dn_out[0], bsz=dbsz, n_chunks=1, c=8)
    o_dn_s = o_dn_s.reshape(dbsz, 8, -1)[:, :dt].reshape(ts, -1)
    conv_p = p[tp - (CONV_WIDTH - 1):tp, C_QKV:C_QKV + DN_CONV_CH]
    conv_s = jnp.concatenate([state_conv[0], qkv_s.reshape(dbsz, dt, -1)], axis=1)[:, -(CONV_WIDTH - 1):]

    o_dn = jnp.concatenate([o_dn_p, o_dn_s], axis=0)
    o_sa = jnp.concatenate([o_sa_p, o_sa_s.astype(BF16)], axis=0)
    merged = _merge(o_dn, o_sa, p, w_o_dn[0].astype(BF16), w_o_sa[0].astype(BF16))
    m2 = _matmul(merged, w_out[0].astype(BF16), F32)
    x1, h2p, logits_t = _post_attn(x_p, x_s, m2, mod_exp, g_norm2[0], jnp.swapaxes(w_router, 1, 2)[0], b_router[0], npt)
    ys, top_w_t = _moe(h2p, logits_t, w_up[0], b_up[0], w_down[0], b_down[0])
    y_p, y_s = _final(x1, mod_exp, top_w_t, ys, npt)

    kv = (SA_KV_HEADS, SA_DIM)
    return (y_p.reshape(x_prompt.shape), y_s.reshape(x_sample.shape),
            ko_p.reshape(1, 1, tp, *kv), vo_p.reshape(1, 1, tp, *kv), kio_p.reshape(1, 1, tp, IDX_DIM),
            conv_p.reshape(1, 1, CONV_WIDTH - 1, DN_CONV_CH), ssm_p.reshape(1, *ssm_p.shape),
            ko_s.reshape(1, dbsz, dt, *kv), vo_s.reshape(1, dbsz, dt, *kv), kio_s.reshape(1, dbsz, dt, IDX_DIM),
            conv_s.reshape(1, dbsz, CONV_WIDTH - 1, DN_CONV_CH), ssm_s.reshape(1, *ssm_s.shape))
```

```python
import functools
import math

import jax
import jax.numpy as jnp
import numpy as np
from jax import lax
from jax.experimental import pallas as pl
from jax.experimental.pallas import tpu as pltpu

D_MODEL = 2048
DN_HEADS = 16
DN_DIM = 128
CONV_WIDTH = 4
DN_CONV_CH = 3 * DN_HEADS * DN_DIM
SA_HEADS = 16
SA_KV_HEADS = 4
SA_GROUP = SA_HEADS // SA_KV_HEADS
SA_DIM = 128
IDX_HEADS = 16
IDX_DIM = 64
TOPK = 256
REL_BUCKETS = 32
REL_MAX_DIST = 128
N_EXPERTS = 32
TOP_K = 4
D_FF = 2048
SWIGLU_LIMIT = 7.0
SWIGLU_ALPHA = 1.702
MOE_BLOCK = 256
PAGE = 128
EPS = 1e-6

F32 = jnp.float32
BF16 = jnp.bfloat16
HI = lax.Precision.HIGHEST

VMEM_LIMIT = 56 * 1024 * 1024
TOK_TILE = 128

C_QKV, C_Z, C_SQ, C_SK, C_SV, C_IQ, C_GD, C_GS, C_BIG = 0, 6144, 8192, 10240, 10752, 11264, 12288, 14336, 16384
S_IK, S_A, S_B, S_IW = 0, 64, 80, 96


def _params(sem, vmem=VMEM_LIMIT):
    return pltpu.CompilerParams(dimension_semantics=sem, vmem_limit_bytes=vmem)


def _dot(a, b):
    return jnp.dot(a, b, preferred_element_type=F32)


def _dot_nt(a, b):
    return lax.dot_general(a, b, (((1,), (1,)), ((), ())), preferred_element_type=F32)


def _split_bf16(a):
    hi = a.astype(BF16)
    lo = (a - hi.astype(F32)).astype(BF16)
    return hi, lo


def _dot3(a, b):
    ah, al = _split_bf16(a)
    bh, bl = _split_bf16(b)
    return _dot(ah, bh) + (_dot(ah, bl) + _dot(al, bh))


def _ada_kernel(c_ref, w_ref, b_ref, o_ref):
    o_ref[...] = jnp.dot(c_ref[...], w_ref[...], precision=HI, preferred_element_type=F32) + b_ref[...]


def _adaln(c_all, w_ada, b_ada):
    r, d = c_all.shape
    n = w_ada.shape[1]
    tn = 1024
    return pl.pallas_call(
        _ada_kernel,
        grid=(n // tn,),
        in_specs=[pl.BlockSpec((r, d), lambda j: (0, 0)),
                  pl.BlockSpec((d, tn), lambda j: (0, j)),
                  pl.BlockSpec((1, tn), lambda j: (0, j))],
        out_specs=pl.BlockSpec((r, tn), lambda j: (0, j)),
        out_shape=jax.ShapeDtypeStruct((r, n), F32),
        compiler_params=_params(("parallel",)),
        name="adaln",
    )(c_all, w_ada, b_ada.reshape(1, n))


def _norm_mod_kernel(xp_ref, xs_ref, g_ref, sh_ref, sc_ref, ws_ref, h_ref, s_ref, *, n_prompt_tiles):
    x = jnp.where(pl.program_id(0) < n_prompt_tiles, xp_ref[...], xs_ref[...])
    y = x * lax.rsqrt(jnp.mean(x * x, axis=-1, keepdims=True) + EPS) * g_ref[...]
    h = y * (1.0 + sc_ref[...]) + sh_ref[...]
    h_ref[...] = h.astype(BF16)
    s_ref[...] = lax.dot_general(h, ws_ref[...], (((1,), (1,)), ((), ())), precision=HI, preferred_element_type=F32)


def _mod_row_block(n_prompt_tiles):
    return lambda i: jnp.maximum(i - n_prompt_tiles + 1, 0)


def _token_specs(d, n_prompt_tiles):
    last = n_prompt_tiles - 1
    return [pl.BlockSpec((TOK_TILE, d), lambda i: (jnp.minimum(i, last), 0)),
            pl.BlockSpec((TOK_TILE, d), lambda i: (jnp.maximum(i - n_prompt_tiles, 0), 0))]


def _norm_mod(x_p, x_s, g, mod_exp, w_small_t, n_prompt_tiles):
    d = x_p.shape[1]
    t = x_p.shape[0] + x_s.shape[0]
    rb = _mod_row_block(n_prompt_tiles)
    return pl.pallas_call(
        functools.partial(_norm_mod_kernel, n_prompt_tiles=n_prompt_tiles),
        grid=(t // TOK_TILE,),
        in_specs=_token_specs(d, n_prompt_tiles) + [
            pl.BlockSpec((1, d), lambda i: (0, 0)),
            pl.BlockSpec((TOK_TILE, d), lambda i: (rb(i), 0)),
            pl.BlockSpec((TOK_TILE, d), lambda i: (rb(i), 1)),
            pl.BlockSpec((128, d), lambda i: (0, 0))],
        out_specs=[pl.BlockSpec((TOK_TILE, d), lambda i: (i, 0)),
                   pl.BlockSpec((TOK_TILE, 128), lambda i: (i, 0))],
        out_shape=[jax.ShapeDtypeStruct((t, d), BF16), jax.ShapeDtypeStruct((t, 128), F32)],
        compiler_params=_params(("parallel",)),
        name="norm_mod",
    )(x_p, x_s, g.reshape(1, d), mod_exp, mod_exp, w_small_t)


def _mm_kernel(x_ref, w_ref, o_ref):
    o_ref[...] = _dot(x_ref[...], w_ref[...]).astype(o_ref.dtype)


def _mm_nt_kernel(x_ref, wt_ref, o_ref):
    o_ref[...] = _dot_nt(x_ref[...], wt_ref[...]).astype(o_ref.dtype)


def _row_tile(m, cap=1024):
    for tm in (1024, 896, 832, 768, 640, 512, 384, 256, 128):
        if tm <= cap and m % tm == 0:
            return tm
    raise ValueError(m)


def _matmul(x, w, out_dtype, tn=512, w_transposed=False):
    m, k = x.shape
    n = w.shape[0] if w_transposed else w.shape[1]
    tm = _row_tile(m)
    w_spec = pl.BlockSpec((tn, k), lambda j, i: (j, 0)) if w_transposed else pl.BlockSpec((k, tn), lambda j, i: (0, j))
    return pl.pallas_call(
        _mm_nt_kernel if w_transposed else _mm_kernel,
        grid=(n // tn, m // tm),
        in_specs=[pl.BlockSpec((tm, k), lambda j, i: (i, 0)), w_spec],
        out_specs=pl.BlockSpec((tm, tn), lambda j, i: (i, j)),
        out_shape=jax.ShapeDtypeStruct((m, n), out_dtype),
        compiler_params=_params(("parallel", "parallel")),
        name="matmul",
    )(x, w)


def _sa_prep_kernel(q_ref, k_ref, v_ref, iq_ref, s_ref, gq_ref, gk_ref,
                    qn_ref, ko_ref, kb_ref, vo_ref, vt_ref, qi_ref, kio_ref, kib_ref, wit_ref):
    gq = gq_ref[...]
    gk = gk_ref[...]
    for h in range(SA_HEADS):
        x = q_ref[:, h * SA_DIM:(h + 1) * SA_DIM]
        y = x * lax.rsqrt(jnp.mean(x * x, axis=-1, keepdims=True) + EPS) * gq
        qn_ref[h] = (y * (SA_DIM ** -0.5 * LOG2E)).astype(BF16)
    for g in range(SA_KV_HEADS):
        x = k_ref[:, g * SA_DIM:(g + 1) * SA_DIM]
        y = x * lax.rsqrt(jnp.mean(x * x, axis=-1, keepdims=True) + EPS) * gk
        ko_ref[:, g * SA_DIM:(g + 1) * SA_DIM] = y
        kb_ref[g] = y.astype(BF16)
        v = v_ref[:, g * SA_DIM:(g + 1) * SA_DIM]
        vt_ref[g, 0] = v.T.astype(BF16)
    vo_ref[...] = v_ref[...]
    for h in range(IDX_HEADS):
        qi_ref[h] = iq_ref[:, h * IDX_DIM:(h + 1) * IDX_DIM].astype(BF16)
    s = s_ref[...]
    ki = s[:, S_IK:S_IK + IDX_DIM]
    kio_ref[...] = ki
    kib_ref[...] = ki.astype(BF16)
    wit_ref[...] = s.T[S_IW:S_IW + IDX_HEADS, :] * ((IDX_HEADS ** -0.5) * (IDX_DIM ** -0.5))


def _sa_prep(p, small, g_q, g_k, row0, t, tm):
    nt = t // tm
    rb = row0 // tm
    col = lambda c0, w: (lambda i: (i + rb, c0 // w))
    out_shape = [
        jax.ShapeDtypeStruct((SA_HEADS, t, SA_DIM), BF16),
        jax.ShapeDtypeStruct((t, SA_KV_HEADS * SA_DIM), F32),
        jax.ShapeDtypeStruct((SA_KV_HEADS, t, SA_DIM), BF16),
        jax.ShapeDtypeStruct((t, SA_KV_HEADS * SA_DIM), F32),
        jax.ShapeDtypeStruct((SA_KV_HEADS, nt, SA_DIM, tm), BF16),
        jax.ShapeDtypeStruct((IDX_HEADS, t, IDX_DIM), BF16),
        jax.ShapeDtypeStruct((t, IDX_DIM), F32),
        jax.ShapeDtypeStruct((t, IDX_DIM), BF16),
        jax.ShapeDtypeStruct((IDX_HEADS, t), F32),
    ]
    out_specs = [
        pl.BlockSpec((SA_HEADS, tm, SA_DIM), lambda i: (0, i, 0)),
        pl.BlockSpec((tm, 512), lambda i: (i, 0)),
        pl.BlockSpec((SA_KV_HEADS, tm, SA_DIM), lambda i: (0, i, 0)),
        pl.BlockSpec((tm, 512), lambda i: (i, 0)),
        pl.BlockSpec((SA_KV_HEADS, 1, SA_DIM, tm), lambda i: (0, i, 0, 0)),
        pl.BlockSpec((IDX_HEADS, tm, IDX_DIM), lambda i: (0, i, 0)),
        pl.BlockSpec((tm, IDX_DIM), lambda i: (i, 0)),
        pl.BlockSpec((tm, IDX_DIM), lambda i: (i, 0)),
        pl.BlockSpec((IDX_HEADS, tm), lambda i: (0, i)),
    ]
    return pl.pallas_call(
        _sa_prep_kernel,
        grid=(nt,),
        in_specs=[pl.BlockSpec((tm, 2048), col(C_SQ, 2048)),
                  pl.BlockSpec((tm, 512), col(C_SK, 512)),
                  pl.BlockSpec((tm, 512), col(C_SV, 512)),
                  pl.BlockSpec((tm, 1024), col(C_IQ, 1024)),
                  pl.BlockSpec((tm, 128), lambda i: (i + rb, 0)),
                  pl.BlockSpec((1, SA_DIM), lambda i: (0, 0)),
                  pl.BlockSpec((1, SA_DIM), lambda i: (0, 0))],
        out_specs=out_specs,
        out_shape=out_shape,
        compiler_params=_params(("parallel",)),
        name="sa_prep",
    )(p, p, p, p, small, g_q.reshape(1, SA_DIM), g_k.reshape(1, SA_DIM))


DSA_TQ = 256
BISECT_ITERS = 40
LOG2E = 1.4426950408889634


def _t5_bucket(dist):
    n = jnp.maximum(dist, 0)
    exact = REL_BUCKETS // 2
    log_ratio = jnp.log(jnp.maximum(n, exact).astype(F32) / exact) / math.log(REL_MAX_DIST / exact)
    large = jnp.minimum(exact + (log_ratio * (REL_BUCKETS - exact)).astype(jnp.int32), REL_BUCKETS - 1)
    return jnp.where(n < exact, n, large)


def _topk_threshold(sc_ref, n_tiles, tk, tq):
    inf = jnp.float32(jnp.inf)

    def tile(c):
        return sc_ref[pl.ds(pl.multiple_of(c * tk, tk), tk), :].reshape(tk // 8, 8, tq)

    def mm_body(c, carry):
        mn, mx = carry
        blk = tile(c)
        return (jnp.minimum(mn, jnp.where(blk == -inf, inf, blk).min(axis=0)), jnp.maximum(mx, blk.max(axis=0)))

    mn8, mx8 = lax.fori_loop(0, n_tiles, mm_body, (jnp.full((8, tq), inf, F32), jnp.full((8, tq), -inf, F32)))
    lo = mn8.min(axis=0, keepdims=True)
    hi = mx8.max(axis=0, keepdims=True)

    def count_ge(v):
        def cnt_body(c, acc):
            return acc + jnp.where(tile(c) >= v, 1.0, 0.0).sum(axis=0)
        return lax.fori_loop(0, n_tiles, cnt_body, jnp.zeros((8, tq), F32)).sum(axis=0, keepdims=True)

    def unfinished(cnt_lo):
        return (jnp.max(cnt_lo) > float(TOPK)).astype(jnp.int32)

    def cond(carry):
        it, go = carry[0], carry[1]
        return (it < BISECT_ITERS) & (go > 0)

    def body(carry):
        it, _, lo, hi, cnt_lo = carry
        mid = 0.5 * (lo + hi)
        cnt = count_ge(mid)
        ok = cnt >= float(TOPK)
        lo = jnp.where(ok, mid, lo)
        cnt_lo = jnp.where(ok, cnt, cnt_lo)
        return it + 1, unfinished(cnt_lo), lo, jnp.where(ok, hi, mid), cnt_lo

    cnt_lo = count_ge(lo)
    carry = lax.while_loop(cond, body, (jnp.int32(0), unfinished(cnt_lo), lo, hi, cnt_lo))
    return carry[2]


def _dsa_prompt_kernel(rb_ref, qi_ref, wit_ref, ki_ref, q_ref, k_ref, vt_ref,
                       o_ref, sc_ref, bias_ref, acc_ref, m_ref, l_ref):
    tq = tk = DSA_TQ
    gw = SA_GROUP * tq
    qb = pl.program_id(0)
    ninf = jnp.float32(-jnp.inf)

    @pl.when(qb == 0)
    def _():
        def bias_rows(rc, carry):
            r0 = pl.multiple_of(rc * 8, 8)
            dist = (lax.broadcasted_iota(jnp.int32, (8, tq), 1) + tk) - (r0 + lax.broadcasted_iota(jnp.int32, (8, tq), 0))
            bucket = _t5_bucket(dist)
            hit = [bucket == i for i in range(REL_BUCKETS)]
            for h in range(SA_HEADS):
                val = jnp.zeros((8, tq), F32)
                for i in range(REL_BUCKETS):
                    val = jnp.where(hit[i], rb_ref[i, h], val)
                j = h % SA_GROUP
                bias_ref[h // SA_GROUP, pl.ds(r0, 8), j * tq:(j + 1) * tq] = (val - rb_ref[REL_BUCKETS - 1, h]) * LOG2E
            return carry

        lax.fori_loop(0, 2 * tk // 8, bias_rows, 0)

    def idx_body(kt, carry):
        r0 = pl.multiple_of(kt * tk, tk)
        ki_t = ki_ref[pl.ds(r0, tk), :]
        for h in range(IDX_HEADS):
            part = wit_ref[h:h + 1, :] * jnp.maximum(_dot_nt(ki_t, qi_ref[h]), 0.0)
            if h == 0:
                sc_ref[pl.ds(r0, tk), :] = part
            else:
                sc_ref[pl.ds(r0, tk), :] += part
        return carry

    lax.fori_loop(0, qb + 1, idx_body, 0)
    d0 = pl.multiple_of(qb * tk, tk)
    causal = lax.broadcasted_iota(jnp.int32, (tk, tq), 0) <= lax.broadcasted_iota(jnp.int32, (tk, tq), 1)
    sc_ref[pl.ds(d0, tk), :] = jnp.where(causal, sc_ref[pl.ds(d0, tk), :], ninf)

    thr = _topk_threshold(sc_ref, qb + 1, tk, tq)

    def mask_body(c, carry):
        r0 = pl.multiple_of(c * tk, tk)
        sc_ref[pl.ds(r0, tk), :] = jnp.where(sc_ref[pl.ds(r0, tk), :] >= thr, 0.0, ninf)
        return carry

    lax.fori_loop(0, qb + 1, mask_body, 0)

    m_ref[...] = jnp.full(m_ref.shape, ninf, F32)
    l_ref[...] = jnp.zeros(l_ref.shape, F32)
    acc_ref[...] = jnp.zeros(acc_ref.shape, F32)

    def tile(kt, bias_row0):
        r0 = pl.multiple_of(kt * tk, tk)
        madd1 = sc_ref[pl.ds(r0, tk), :]
        madd = jnp.concatenate([madd1] * SA_GROUP, axis=1)
        for g in range(SA_KV_HEADS):
            qg = q_ref[g * SA_GROUP:(g + 1) * SA_GROUP].reshape(gw, SA_DIM)
            s = _dot_nt(k_ref[g, pl.ds(r0, tk), :], qg) + madd
            if bias_row0 is not None:
                s = s + bias_ref[g, bias_row0:bias_row0 + tk, :]
            m_old = m_ref[g]
            m_new = jnp.maximum(m_old, s.max(axis=0, keepdims=True))
            m_safe = jnp.where(m_new == ninf, 0.0, m_new)
            p = jnp.exp2(s - m_safe)
            alpha = jnp.exp2(m_old - m_safe)
            l_ref[g] = l_ref[g] * alpha + p.sum(axis=0, keepdims=True)
            acc_ref[g] = acc_ref[g] * alpha + _dot(vt_ref[g, kt], p.astype(BF16))
            m_ref[g] = m_new

    def far_body(kt, carry):
        tile(kt, None)
        return carry

    lax.fori_loop(0, jnp.maximum(qb - 1, 0), far_body, 0)

    @pl.when(qb > 0)
    def _():
        tile(qb - 1, 0)

    tile(qb, tk)

    for h in range(SA_HEADS):
        g, j = divmod(h, SA_GROUP)
        out_t = acc_ref[g, :, j * tq:(j + 1) * tq] / l_ref[g, :, j * tq:(j + 1) * tq]
        o_ref[:, h * SA_DIM:(h + 1) * SA_DIM] = out_t.T.astype(BF16)


def _resident(shape):
    nd = len(shape)
    return pl.BlockSpec(shape, lambda i: (0,) * nd, pipeline_mode=pl.Buffered(1))


def _dsa_prompt(qn, kb, vt, qi, kib, wit, rel_bias):
    t = qn.shape[1]
    tq = DSA_TQ
    assert t % tq == 0 and tq >= REL_MAX_DIST
    gw = SA_GROUP * tq
    return pl.pallas_call(
        _dsa_prompt_kernel,
        grid=(t // tq,),
        in_specs=[pl.BlockSpec(memory_space=pltpu.SMEM),
                  pl.BlockSpec((IDX_HEADS, tq, IDX_DIM), lambda i: (0, i, 0)),
                  pl.BlockSpec((IDX_HEADS, tq), lambda i: (0, i)),
                  _resident(kib.shape),
                  pl.BlockSpec((SA_HEADS, tq, SA_DIM), lambda i: (0, i, 0)),
                  _resident(kb.shape),
                  _resident(vt.shape)],
        out_specs=pl.BlockSpec((tq, SA_HEADS * SA_DIM), lambda i: (i, 0)),
        out_shape=jax.ShapeDtypeStruct((t, SA_HEADS * SA_DIM), BF16),
        scratch_shapes=[pltpu.VMEM((t, tq), F32),
                        pltpu.VMEM((SA_KV_HEADS, 2 * tq, gw), F32),
                        pltpu.VMEM((SA_KV_HEADS, SA_DIM, gw), F32),
                        pltpu.VMEM((SA_KV_HEADS, 1, gw), F32),
                        pltpu.VMEM((SA_KV_HEADS, 1, gw), F32)],
        compiler_params=_params(("arbitrary",)),
        name="dsa_prompt",
    )(rel_bias.astype(F32), qi, wit, kib, qn, kb, vt)


def _dn_prep_kernel(x_ref, halo_ref, cw_ref, s_ref, alog_ref, dtb_ref, o_ref, beta_ref, g_ref,
                    *, tm, n_valid, zero_first_halo):
    x = x_ref[...]
    halo = halo_ref[...]
    if zero_first_halo:
        halo = jnp.where(pl.program_id(0) == 0, 0.0, halo)
    xcat = jnp.concatenate([halo, x], axis=0)
    cw = cw_ref[...]
    conv = x * cw[CONV_WIDTH - 1:CONV_WIDTH, :]
    for j in range(CONV_WIDTH - 1):
        lo = 8 - (CONV_WIDTH - 1) + j
        conv = conv + xcat[lo:lo + tm, :] * cw[j:j + 1, :]
    act = conv * jax.nn.sigmoid(conv)
    valid = None
    if n_valid < tm:
        valid = lax.broadcasted_iota(jnp.int32, (tm, 1), 0) < n_valid
    nqk = 2 * DN_HEADS
    for h in range(3 * DN_HEADS):
        a = act[:, h * DN_DIM:(h + 1) * DN_DIM]
        if h < nqk:
            a = a * lax.rsqrt(jnp.sum(a * a, axis=-1, keepdims=True) + EPS)
            if h < DN_HEADS:
                a = a * (DN_DIM ** -0.5)
        if valid is not None:
            a = jnp.where(valid, a, 0.0)
        o_ref[:, h * DN_DIM:(h + 1) * DN_DIM] = a
    s = s_ref[...]
    a_raw = s[:, S_A:S_A + DN_HEADS] + dtb_ref[...]
    softplus = jnp.maximum(a_raw, 0.0) + jnp.log1p(jnp.exp(-jnp.abs(a_raw)))
    g = -jnp.exp(alog_ref[...]) * softplus
    beta = jax.nn.sigmoid(s[:, S_B:S_B + DN_HEADS])
    if valid is not None:
        g = jnp.where(valid, g, 0.0)
        beta = jnp.where(valid, beta, 0.0)
    g_ref[...] = g
    beta_ref[...] = beta


def _dn_prep(x, x_col0, halo, halo_index, small, conv_w, a_log, dt_bias, *, tm, n_valid, zero_first_halo):
    t = small.shape[0]
    c = DN_CONV_CH
    cb = x_col0 // c
    kern = functools.partial(_dn_prep_kernel, tm=tm, n_valid=n_valid, zero_first_halo=zero_first_halo)
    return pl.pallas_call(
        kern,
        grid=(t // tm,),
        in_specs=[pl.BlockSpec((tm, c), lambda i: (i, cb)),
                  pl.BlockSpec((8, c), halo_index),
                  pl.BlockSpec((CONV_WIDTH, c), lambda i: (0, 0)),
                  pl.BlockSpec((tm, 128), lambda i: (i, 0)),
                  pl.BlockSpec((1, DN_HEADS), lambda i: (0, 0)),
                  pl.BlockSpec((1, DN_HEADS), lambda i: (0, 0))],
        out_specs=[pl.BlockSpec((tm, c), lambda i: (i, 0)),
                   pl.BlockSpec((tm, DN_HEADS), lambda i: (i, 0)),
                   pl.BlockSpec((tm, DN_HEADS), lambda i: (i, 0))],
        out_shape=[jax.ShapeDtypeStruct((t, c), F32),
                   jax.ShapeDtypeStruct((t, DN_HEADS), F32),
                   jax.ShapeDtypeStruct((t, DN_HEADS), F32)],
        compiler_params=_params(("parallel",)),
        name="dn_prep",
    )(x, halo, conv_w, small, a_log.reshape(1, DN_HEADS), dt_bias.reshape(1, DN_HEADS))


DN_SOLVE_BLOCK = 16


def _dn_scan_kernel(x_ref, z_ref, beta_ref, g_ref, gt_ref, s0_ref, gout_ref, o_ref, sfin_ref,
                    s_ref, a_ref, t_ref, n_ref, rhs_ref, attn_ref, vnew_ref, *, c):
    n = pl.program_id(1)
    heads = range(DN_HEADS)

    @pl.when(n == 0)
    def _():
        s_ref[...] = s0_ref[0]

    row = lax.broadcasted_iota(jnp.int32, (c, c), 0)
    col = lax.broadcasted_iota(jnp.int32, (c, c), 1)
    incl = row >= col
    strict = row > col
    eye = (row == col).astype(F32)
    sb = min(c, DN_SOLVE_BLOCK)
    same_blk = (row // sb) == (col // sb)
    g_col = g_ref[...]
    g_row = gt_ref[:, 0, 0, :]
    gc_col = jnp.dot(incl.astype(F32), g_col, precision=HI, preferred_element_type=F32)
    gc_row = jnp.dot(g_row, (row <= col).astype(F32), precision=HI, preferred_element_type=F32)
    beta = beta_ref[...]
    gout = gout_ref[...]
    qs = lambda h: x_ref[:, h * DN_DIM:(h + 1) * DN_DIM]
    ks = lambda h: x_ref[:, (DN_HEADS + h) * DN_DIM:(DN_HEADS + h + 1) * DN_DIM]
    vs = lambda h: x_ref[:, (2 * DN_HEADS + h) * DN_DIM:(2 * DN_HEADS + h + 1) * DN_DIM]

    for h in heads:
        q, k, v = qs(h), ks(h), vs(h)
        gcc = gc_col[:, h:h + 1]
        gcr = gc_row[h:h + 1, :]
        bcol = beta[:, h:h + 1]
        decay = jnp.where(incl, jnp.exp(jnp.where(incl, gcc - gcr, 0.0)), 0.0)
        kb = k * bcol
        kt = k.T
        a = jnp.where(strict, -(_dot3(kb, kt) * decay), 0.0)
        a_diag = jnp.where(same_blk, a, 0.0)
        a_ref[h] = a_diag
        t_ref[h] = eye + a_diag
        n_ref[h] = a - a_diag
        rhs_ref[h] = jnp.concatenate([v * bcol, kb * jnp.exp(gcc)], axis=1)
        attn_ref[h] = jnp.where(incl, _dot3(q, kt) * decay, 0.0)

    for _ in range(int(math.log2(sb)) - 1):
        for h in heads:
            p = a_ref[h]
            p2 = _dot3(p, p)
            a_ref[h] = p2
            t = t_ref[h]
            t_ref[h] = t + _dot3(t, p2)

    for i in range(c // sb):
        rows = slice(i * sb, (i + 1) * sb)
        if i > 0:
            for h in heads:
                rhs_ref[h, rows, :] = rhs_ref[h, rows, :] + _dot3(n_ref[h, rows, :], rhs_ref[h])
        for h in heads:
            rhs_ref[h, rows, :] = _dot3(t_ref[h, rows, :], rhs_ref[h])

    for h in heads:
        vnew_ref[h] = rhs_ref[h, :, :DN_DIM] - _dot3(rhs_ref[h, :, DN_DIM:], s_ref[h])

    for h in heads:
        q, k = qs(h), ks(h)
        gcc = gc_col[:, h:h + 1]
        state = s_ref[h]
        v_new = vnew_ref[h]
        out = _dot3(q * jnp.exp(gcc), state) + _dot3(attn_ref[h], v_new)
        g_last = gcc[c - 1:c, :]
        k_dec = k * jnp.exp(g_last - gcc)
        s_ref[h] = state * jnp.exp(g_last) + _dot3(k_dec.T, v_new)
        y = out * lax.rsqrt(jnp.mean(out * out, axis=-1, keepdims=True) + EPS) * gout
        z = z_ref[:, h * DN_DIM:(h + 1) * DN_DIM]
        o_ref[:, h * DN_DIM:(h + 1) * DN_DIM] = (y * (z * jax.nn.sigmoid(z))).astype(o_ref.dtype)

    @pl.when(n == pl.num_programs(1) - 1)
    def _():
        sfin_ref[0] = s_ref[...]


def _dn_scan(xc, z, z_col0, beta, g, s0, g_out, *, bsz, n_chunks, c):
    t = xc.shape[0]
    hdim = DN_HEADS * DN_DIM
    zb = z_col0 // hdim
    gt = g.T.reshape(DN_HEADS, bsz * n_chunks, 1, c)
    kern = functools.partial(_dn_scan_kernel, c=c)
    return pl.pallas_call(
        kern,
        grid=(bsz, n_chunks),
        in_specs=[pl.BlockSpec((c, DN_CONV_CH), lambda b, n: (b * n_chunks + n, 0)),
                  pl.BlockSpec((c, hdim), lambda b, n: (b * n_chunks + n, zb)),
                  pl.BlockSpec((c, DN_HEADS), lambda b, n: (b * n_chunks + n, 0)),
                  pl.BlockSpec((c, DN_HEADS), lambda b, n: (b * n_chunks + n, 0)),
                  pl.BlockSpec((DN_HEADS, 1, 1, c), lambda b, n: (0, b * n_chunks + n, 0, 0)),
                  pl.BlockSpec((1, DN_HEADS, DN_DIM, DN_DIM), lambda b, n: (b, 0, 0, 0)),
                  pl.BlockSpec((1, DN_DIM), lambda b, n: (0, 0))],
        out_specs=[pl.BlockSpec((c, hdim), lambda b, n: (b * n_chunks + n, 0)),
                   pl.BlockSpec((1, DN_HEADS, DN_DIM, DN_DIM), lambda b, n: (b, 0, 0, 0))],
        out_shape=[jax.ShapeDtypeStruct((t, hdim), BF16),
                   jax.ShapeDtypeStruct((bsz, DN_HEADS, DN_DIM, DN_DIM), F32)],
        scratch_shapes=[pltpu.VMEM((DN_HEADS, DN_DIM, DN_DIM), F32),
                        pltpu.VMEM((DN_HEADS, c, c), F32),
                        pltpu.VMEM((DN_HEADS, c, c), F32),
                        pltpu.VMEM((DN_HEADS, c, c), F32),
                        pltpu.VMEM((DN_HEADS, c, 2 * DN_DIM), F32),
                        pltpu.VMEM((DN_HEADS, c, c), F32),
                        pltpu.VMEM((DN_HEADS, c, DN_DIM), F32)],
        compiler_params=_params(("parallel", "arbitrary")),
        name="dn_scan",
    )(xc, z, beta, g, gt, s0, g_out.reshape(1, DN_DIM))


S1_PAGES = 8
S3_PAGES = 4
DEC_T = 4


def _dsa_s_scores_kernel(pt_ref, qi_ref, wsel_ref, kin_ref, *refs):
    pages = refs[:S1_PAGES]
    sc_ref, scn_ref = refs[S1_PAGES:]
    qi = qi_ref[0]
    wsel = wsel_ref[0]

    def score(keys_t):
        r = jnp.maximum(_dot(qi, keys_t.astype(BF16)), 0.0)
        return jnp.dot(wsel, r, precision=HI, preferred_element_type=F32)

    for i in range(S1_PAGES):
        sc_ref[0, :, i * PAGE:(i + 1) * PAGE] = score(pages[i][0])
    sn = score(kin_ref[0])
    t_idx = lax.broadcasted_iota(jnp.int32, sn.shape, 0)
    c_idx = lax.broadcasted_iota(jnp.int32, sn.shape, 1)
    scn_ref[0] = jnp.where((c_idx <= t_idx) & (c_idx < DEC_T), sn, -jnp.inf)


def _dsa_s_scores(page_table, qi_s, wsel, ki_new_t, cache_kidx_t):
    dbsz, n_pages = page_table.shape
    past = n_pages * PAGE
    steps = n_pages // S1_PAGES
    page_spec = lambda i: pl.BlockSpec((1, IDX_DIM, PAGE), lambda b, j, pt: (pt[b * n_pages + j * S1_PAGES + i], 0, 0))
    grid_spec = pltpu.PrefetchScalarGridSpec(
        num_scalar_prefetch=1,
        grid=(dbsz, steps),
        in_specs=[pl.BlockSpec((1,) + qi_s.shape[1:], lambda b, j, pt: (b, 0, 0)),
                  pl.BlockSpec((1,) + wsel.shape[1:], lambda b, j, pt: (b, 0, 0)),
                  pl.BlockSpec((1, IDX_DIM, PAGE), lambda b, j, pt: (b, 0, 0))]
                 + [page_spec(i) for i in range(S1_PAGES)],
        out_specs=[pl.BlockSpec((1, 8, S1_PAGES * PAGE), lambda b, j, pt: (b, 0, j)),
                   pl.BlockSpec((1, 8, PAGE), lambda b, j, pt: (b, 0, 0))],
    )
    return pl.pallas_call(
        _dsa_s_scores_kernel,
        grid_spec=grid_spec,
        out_shape=[jax.ShapeDtypeStruct((dbsz, 8, past), F32), jax.ShapeDtypeStruct((dbsz, 8, PAGE), F32)],
        compiler_params=_params(("parallel", "arbitrary")),
        name="dsa_sample_scores",
    )(page_table.reshape(-1), qi_s, wsel, ki_new_t, *([cache_kidx_t] * S1_PAGES))


def _dsa_s_mask_kernel(sc_ref, scn_ref, m_ref, mn_ref):
    inf = jnp.float32(jnp.inf)

    def row_min(s):
        return jnp.where(s == -inf, inf, s).min(axis=1, keepdims=True)

    lo = jnp.minimum(row_min(sc_ref[...]), row_min(scn_ref[...]))
    hi = jnp.maximum(sc_ref[...].max(axis=1, keepdims=True), scn_ref[...].max(axis=1, keepdims=True))

    def count(s, mid):
        return jnp.where(s >= mid, 1.0, 0.0).sum(axis=1, keepdims=True)

    def bis_body(_, carry):
        lo, hi = carry
        mid = 0.5 * (lo + hi)
        ok = (count(sc_ref[...], mid) + count(scn_ref[...], mid)) >= float(TOPK)
        return jnp.where(ok, mid, lo), jnp.where(ok, hi, mid)

    thr, _ = lax.fori_loop(0, BISECT_ITERS, bis_body, (lo, hi))
    m_ref[...] = jnp.where(sc_ref[...] >= thr, 0.0, -inf)
    mn_ref[...] = jnp.where(scn_ref[...] >= thr, 0.0, -inf)


def _dsa_s_mask(sc, scn):
    rows, past = sc.shape
    return pl.pallas_call(
        _dsa_s_mask_kernel,
        out_shape=[jax.ShapeDtypeStruct((rows, past), F32), jax.ShapeDtypeStruct((rows, PAGE), F32)],
        compiler_params=_params(None),
        name="dsa_sample_mask",
    )(sc, scn)


def _dsa_s_attn_kernel(pt_ref, q_ref, m_ref, mn_ref, kn_ref, vn_ref, hm_ref, bias_ref, *refs, n_steps):
    kp = refs[:S3_PAGES]
    vp = refs[S3_PAGES:2 * S3_PAGES]
    o_ref, acc_ref, mx_ref, l_ref = refs[2 * S3_PAGES:]
    j = pl.program_id(1)
    ninf = jnp.float32(-jnp.inf)
    nrow = SA_HEADS * DEC_T
    pcols = PAGE * SA_KV_HEADS
    q = q_ref[0]

    def update(s, v_bf16, first):
        m_old = jnp.full((nrow, 1), ninf, F32) if first else mx_ref[:, 0:1]
        m_new = jnp.maximum(m_old, s.max(axis=1, keepdims=True))
        m_safe = jnp.where(m_new == ninf, 0.0, m_new)
        p = jnp.exp2(s - m_safe)
        pv = _dot(p.astype(BF16), v_bf16)
        if first:
            l_new = p.sum(axis=1, keepdims=True)
            acc_new = pv
        else:
            alpha = jnp.exp2(m_old - m_safe)
            l_new = l_ref[:, 0:1] * alpha + p.sum(axis=1, keepdims=True)
            acc_new = acc_ref[...] * alpha + pv
        mx_ref[...] = jnp.broadcast_to(m_new, (nrow, 128))
        l_ref[...] = jnp.broadcast_to(l_new, (nrow, 128))
        acc_ref[...] = acc_new

    def expand(mask_t):
        return jnp.concatenate([mask_t[0:DEC_T]] * SA_HEADS, axis=0)

    hm = hm_ref[...]

    @pl.when(j == 0)
    def _():
        s = _dot_nt(q, kn_ref[0]) + bias_ref[:, pcols:pcols + PAGE] + hm[:, :PAGE] + expand(mn_ref[0])
        update(s, vn_ref[0], True)

    is_last = j == n_steps - 1
    k = jnp.concatenate([kp[i][0] for i in range(S3_PAGES)], axis=0).astype(BF16)
    v = jnp.concatenate([vp[i][0] for i in range(S3_PAGES)], axis=0).astype(BF16)
    tail = jnp.where(is_last, bias_ref[:, 0:pcols], 0.0)
    bias = jnp.concatenate([hm] * (S3_PAGES - 1) + [hm + tail], axis=1)
    s = _dot_nt(q, k) + bias + expand(m_ref[0])
    update(s, v, False)

    @pl.when(is_last)
    def _():
        o_ref[0] = acc_ref[...] / l_ref[...]


def _dsa_s_attn(page_table, q_s, madd_x, madd_new_x, k_new, v_new, head_mask, bias, cache_k, cache_v):
    dbsz, n_pages = page_table.shape
    n_steps = n_pages // S3_PAGES
    nrow = SA_HEADS * DEC_T
    pcols = PAGE * SA_KV_HEADS
    page_spec = lambda i: pl.BlockSpec((1, pcols, SA_DIM), lambda b, j, pt: (pt[b * n_pages + j * S3_PAGES + i], 0, 0))
    grid_spec = pltpu.PrefetchScalarGridSpec(
        num_scalar_prefetch=1,
        grid=(dbsz, n_steps),
        in_specs=[pl.BlockSpec((1, nrow, SA_DIM), lambda b, j, pt: (b, 0, 0)),
                  pl.BlockSpec((1, 8, S3_PAGES * pcols), lambda b, j, pt: (b, 0, j)),
                  pl.BlockSpec((1, 8, PAGE), lambda b, j, pt: (b, 0, 0)),
                  pl.BlockSpec((1, PAGE, SA_DIM), lambda b, j, pt: (b, 0, 0)),
                  pl.BlockSpec((1, PAGE, SA_DIM), lambda b, j, pt: (b, 0, 0)),
                  pl.BlockSpec((nrow, pcols), lambda b, j, pt: (0, 0)),
                  pl.BlockSpec((nrow, pcols + PAGE), lambda b, j, pt: (0, 0))]
                 + [page_spec(i) for i in range(S3_PAGES)] * 2,
        out_specs=pl.BlockSpec((1, nrow, SA_DIM), lambda b, j, pt: (b, 0, 0)),
        scratch_shapes=[pltpu.VMEM((nrow, SA_DIM), F32), pltpu.VMEM((nrow, 128), F32), pltpu.VMEM((nrow, 128), F32)],
    )
    return pl.pallas_call(
        functools.partial(_dsa_s_attn_kernel, n_steps=n_steps),
        grid_spec=grid_spec,
        out_shape=jax.ShapeDtypeStruct((dbsz, nrow, SA_DIM), F32),
        compiler_params=_params(("parallel", "arbitrary")),
        name="dsa_sample_attn",
    )(page_table.reshape(-1), q_s, madd_x, madd_new_x, k_new, v_new, head_mask, bias,
      *([cache_k] * S3_PAGES), *([cache_v] * S3_PAGES))


def _dsa_sample(qn, ko, vo, qi, kio, wit, cache_k, cache_v, cache_kidx_t, page_table, rel_bias):
    dbsz, n_pages = page_table.shape
    t = DEC_T
    nkv = SA_KV_HEADS
    past = n_pages * PAGE
    assert n_pages % S1_PAGES == 0 and n_pages % S3_PAGES == 0 and PAGE >= REL_MAX_DIST and PAGE >= nkv * t
    n_pool = cache_k.shape[0]
    qi_s = qi.reshape(IDX_HEADS, dbsz, t, IDX_DIM).transpose(1, 2, 0, 3).reshape(dbsz, t * IDX_HEADS, IDX_DIM)
    w_bth = wit.reshape(IDX_HEADS, dbsz, t).transpose(1, 2, 0)
    wsel = (w_bth[:, :, None, :] * jnp.eye(t, dtype=F32)[None, :, :, None]).reshape(dbsz, t, t * IDX_HEADS)
    wsel = jnp.pad(wsel, ((0, 0), (0, 8 - t), (0, 0)))
    ki_new_t = jnp.pad(kio.reshape(dbsz, t, IDX_DIM).transpose(0, 2, 1), ((0, 0), (0, 0), (0, PAGE - t)))
    sc, scn = _dsa_s_scores(page_table, qi_s, wsel, ki_new_t, cache_kidx_t)
    madd, madd_new = _dsa_s_mask(sc.reshape(dbsz * 8, past), scn.reshape(dbsz * 8, PAGE))
    madd_x = jnp.repeat(madd.reshape(dbsz, 8, past), nkv, axis=-1)
    madd_new_x = jnp.repeat(madd_new.reshape(dbsz, 8, PAGE)[:, :, :PAGE // nkv], nkv, axis=-1)
    new_rows = lambda a: jnp.pad(a.reshape(dbsz, t * nkv, SA_DIM), ((0, 0), (0, PAGE - t * nkv), (0, 0))).astype(BF16)
    q_s = qn.reshape(nkv, SA_GROUP, dbsz, t, SA_DIM).transpose(2, 0, 1, 3, 4).reshape(dbsz, SA_HEADS * t, SA_DIM)
    nrow = SA_HEADS * t
    r = jnp.arange(nrow, dtype=jnp.int32)
    c = jnp.arange(PAGE * nkv, dtype=jnp.int32)
    head_mask = jnp.where((c[None, :] % nkv) == (r[:, None] // (SA_GROUP * t)), 0.0, -jnp.inf).astype(F32)
    dist = jnp.concatenate([PAGE + (r % t)[:, None] - (c // nkv)[None, :],
                            (r % t)[:, None] - (c[:PAGE] // nkv)[None, :]], axis=1)
    rb_rows = jnp.repeat(rel_bias.astype(F32).T, t, axis=0)
    onehot = _t5_bucket(dist)[:, :, None] == jnp.arange(REL_BUCKETS, dtype=jnp.int32)
    bias = (jnp.sum(jnp.where(onehot, rb_rows[:, None, :], 0.0), axis=-1) - rb_rows[:, REL_BUCKETS - 1:]) * LOG2E
    o = _dsa_s_attn(page_table, q_s, madd_x, madd_new_x, new_rows(ko), new_rows(vo), head_mask, bias,
                    cache_k.reshape(n_pool, PAGE * nkv, SA_DIM), cache_v.reshape(n_pool, PAGE * nkv, SA_DIM))
    return o.reshape(dbsz, nkv, SA_GROUP, t, SA_DIM).transpose(0, 3, 1, 2, 4).reshape(dbsz * t, SA_HEADS * SA_DIM)


def _merge_kernel(odn_ref, osa_ref, gd_ref, gs_ref, w1_ref, w2_ref, o_ref):
    a = _dot(odn_ref[...], w1_ref[...])
    b = _dot(osa_ref[...], w2_ref[...])
    o_ref[...] = (jax.nn.sigmoid(gd_ref[...]) * a + jax.nn.sigmoid(gs_ref[...]) * b).astype(o_ref.dtype)


def _merge(o_dn, o_sa, p, w1, w2, tn=512):
    m, k = o_dn.shape
    n = w1.shape[1]
    tm = _row_tile(m)
    return pl.pallas_call(
        _merge_kernel,
        grid=(n // tn, m // tm),
        in_specs=[pl.BlockSpec((tm, k), lambda j, i: (i, 0)),
                  pl.BlockSpec((tm, k), lambda j, i: (i, 0)),
                  pl.BlockSpec((tm, tn), lambda j, i: (i, C_GD // tn + j)),
                  pl.BlockSpec((tm, tn), lambda j, i: (i, C_GS // tn + j)),
                  pl.BlockSpec((k, tn), lambda j, i: (0, j)),
                  pl.BlockSpec((k, tn), lambda j, i: (0, j))],
        out_specs=pl.BlockSpec((tm, tn), lambda j, i: (i, j)),
        out_shape=jax.ShapeDtypeStruct((m, n), BF16),
        compiler_params=_params(("parallel", "parallel")),
        name="merge",
    )(o_dn, o_sa, p, p, w1, w2)


def _pack_bf16_pair(lo, hi):
    lo_bits = pltpu.bitcast(lo.astype(BF16).astype(F32), jnp.uint32)
    hi_bits = pltpu.bitcast(hi.astype(BF16).astype(F32), jnp.uint32)
    return (hi_bits & jnp.uint32(0xFFFF0000)) | (lo_bits >> 16)


def _unpack_bf16_pair(w):
    lo = pltpu.bitcast(w << 16, F32)
    hi = pltpu.bitcast(w & jnp.uint32(0xFFFF0000), F32)
    return jnp.concatenate([lo, hi], axis=1).astype(BF16)


def _post_attn_kernel(xp_ref, xs_ref, m_ref, g1_ref, sh_ref, sc_ref, gn_ref, wr_ref, br_ref, x1_ref, hp_ref, lg_ref,
                      *, n_prompt_tiles):
    x = jnp.where(pl.program_id(0) < n_prompt_tiles, xp_ref[...], xs_ref[...])
    x1 = x + g1_ref[...] * m_ref[...]
    x1_ref[...] = x1
    y = x1 * lax.rsqrt(jnp.mean(x1 * x1, axis=-1, keepdims=True) + EPS) * gn_ref[...]
    h2 = y * (1.0 + sc_ref[...]) + sh_ref[...]
    half = h2.shape[1] // 2
    hp_ref[...] = _pack_bf16_pair(h2[:, :half], h2[:, half:])
    lg_ref[...] = lax.dot_general(wr_ref[...], h2, (((1,), (1,)), ((), ())), precision=HI,
                                  preferred_element_type=F32) + br_ref[...]


def _post_attn(x_p, x_s, m2, mod_exp, g_norm2, w_router_t, b_router, n_prompt_tiles):
    t, d = m2.shape
    rb = _mod_row_block(n_prompt_tiles)
    tile = lambda c: pl.BlockSpec((TOK_TILE, d), lambda i: (rb(i), c))
    return pl.pallas_call(
        functools.partial(_post_attn_kernel, n_prompt_tiles=n_prompt_tiles),
        grid=(t // TOK_TILE,),
        in_specs=_token_specs(d, n_prompt_tiles) + [
            pl.BlockSpec((TOK_TILE, d), lambda i: (i, 0)),
            tile(2), tile(3), tile(4),
            pl.BlockSpec((1, d), lambda i: (0, 0)),
            pl.BlockSpec((N_EXPERTS, d), lambda i: (0, 0)),
            pl.BlockSpec((N_EXPERTS, 1), lambda i: (0, 0))],
        out_specs=[pl.BlockSpec((TOK_TILE, d), lambda i: (i, 0)),
                   pl.BlockSpec((TOK_TILE, d // 2), lambda i: (i, 0)),
                   pl.BlockSpec((N_EXPERTS, TOK_TILE), lambda i: (0, i))],
        out_shape=[jax.ShapeDtypeStruct((t, d), F32),
                   jax.ShapeDtypeStruct((t, d // 2), jnp.uint32),
                   jax.ShapeDtypeStruct((N_EXPERTS, t), F32)],
        compiler_params=_params(("parallel",)),
        name="post_attn",
    )(x_p, x_s, m2, mod_exp, mod_exp, mod_exp, g_norm2.reshape(1, d), w_router_t, b_router.reshape(N_EXPERTS, 1))


def _route_kernel(lg_ref, e_ref, w_ref, r_ref, cnt_ref, run_ref):
    @pl.when(pl.program_id(0) == 0)
    def _():
        run_ref[...] = jnp.zeros(run_ref.shape, F32)

    tt = TOK_TILE
    lg = lg_ref[...]
    e_iota = lax.broadcasted_iota(jnp.int32, lg.shape, 0)
    sels, tops = [], []
    for r in range(TOP_K):
        m = lg.max(axis=0, keepdims=True)
        e = jnp.where(lg == m, e_iota, N_EXPERTS).min(axis=0, keepdims=True)
        sel = e_iota == e
        lg = jnp.where(sel, -jnp.inf, lg)
        e_ref[r:r + 1, :] = e
        sels.append(sel)
        tops.append(m)
    ex = [jnp.exp(m - tops[0]) for m in tops]
    denom = ex[0] + ex[1] + ex[2] + ex[3]
    for r in range(TOP_K):
        w_ref[r:r + 1, :] = ex[r] / denom
    onehot = sels[0] | sels[1] | sels[2] | sels[3]
    oh = jnp.where(onehot, 1.0, 0.0).astype(BF16)
    before = (lax.broadcasted_iota(jnp.int32, (tt, tt), 0) < lax.broadcasted_iota(jnp.int32, (tt, tt), 1))
    prefix = _dot(oh, jnp.where(before, 1.0, 0.0).astype(BF16))
    total = _dot(oh, jnp.ones((tt, tt), BF16))
    base = run_ref[...] + prefix
    for r in range(TOP_K):
        r_ref[r:r + 1, :] = jnp.where(sels[r], base, 0.0).sum(axis=0, keepdims=True).astype(jnp.int32)
    run_ref[...] = run_ref[...] + total
    cnt_ref[...] = run_ref[...]


def _route(logits_t):
    e, t = logits_t.shape
    spec4 = pl.BlockSpec((TOP_K, TOK_TILE), lambda i: (0, i))
    return pl.pallas_call(
        _route_kernel,
        grid=(t // TOK_TILE,),
        in_specs=[pl.BlockSpec((e, TOK_TILE), lambda i: (0, i))],
        out_specs=[spec4, spec4, spec4, pl.BlockSpec((e, TOK_TILE), lambda i: (0, 0))],
        out_shape=[jax.ShapeDtypeStruct((TOP_K, t), jnp.int32),
                   jax.ShapeDtypeStruct((TOP_K, t), F32),
                   jax.ShapeDtypeStruct((TOP_K, t), jnp.int32),
                   jax.ShapeDtypeStruct((e, TOK_TILE), F32)],
        scratch_shapes=[pltpu.VMEM((e, TOK_TILE), F32)],
        compiler_params=_params(("arbitrary",)),
        name="route",
    )(logits_t)


def _dispatch_kernel(idx_ref, h_ref, o_ref):
    def body(r, carry):
        o_ref[pl.ds(r, 1), :] = h_ref[pl.ds(idx_ref[r], 1), :]
        return carry

    lax.fori_loop(0, MOE_BLOCK, body, 0, unroll=8)


def _dispatch(row_tok, h2p):
    rows = row_tok.shape[0]
    t, w = h2p.shape
    return pl.pallas_call(
        _dispatch_kernel,
        grid=(rows // MOE_BLOCK,),
        in_specs=[pl.BlockSpec((MOE_BLOCK,), lambda i: (i,), memory_space=pltpu.SMEM),
                  pl.BlockSpec((t, w), lambda i: (0, 0), pipeline_mode=pl.Buffered(1))],
        out_specs=pl.BlockSpec((MOE_BLOCK, w), lambda i: (i, 0)),
        out_shape=jax.ShapeDtypeStruct((rows, w), jnp.uint32),
        compiler_params=_params(("parallel",)),
        name="moe_dispatch",
    )(row_tok, h2p)


MOE_GROUP = 6
FF_TILE = 512
N_FF = D_FF // FF_TILE
ST_BLK, ST_F, ST_E, ST_J, ST_CAST, ST_VALID, ST_DONE = range(7)


def _expert_kernel(st_ref, dest_ref, x_ref, wg_ref, wu_ref, wd_ref, bg_ref, bu_ref, bd_ref, ys_ref,
                   wgb_ref, wub_ref, wdb_ref, acc_ref, stage_ref, sem_ref, inflight_ref):
    s = pl.program_id(0)
    f = st_ref[ST_F, s]
    j = st_ref[ST_J, s]

    def row_copy(slot, r, dst_row):
        return pltpu.make_async_copy(stage_ref.at[slot, pl.ds(r, 1)], ys_ref.at[pl.ds(dst_row, 1)], sem_ref.at[slot])

    def drain(slot):
        def body(r, carry):
            row_copy(slot, 0, 0).wait()
            return carry
        lax.fori_loop(0, inflight_ref[slot], body, 0)
        inflight_ref[slot] = 0

    @pl.when(s == 0)
    def _():
        inflight_ref[0] = 0
        inflight_ref[1] = 0

    @pl.when(st_ref[ST_CAST, s] == 1)
    def _():
        wgb_ref[...] = wg_ref[0].astype(BF16)
        wub_ref[...] = wu_ref[0].astype(BF16)
        wdb_ref[...] = wd_ref[0].astype(BF16)

    @pl.when(st_ref[ST_VALID, s] == 1)
    def _():
        x = _unpack_bf16_pair(x_ref[...])
        gate = jnp.minimum(_dot(x, wgb_ref[...]) + bg_ref[0], SWIGLU_LIMIT)
        up = jnp.clip(_dot(x, wub_ref[...]) + bu_ref[0], -SWIGLU_LIMIT, SWIGLU_LIMIT)
        act = (up + 1.0) * gate * jax.nn.sigmoid(SWIGLU_ALPHA * gate)
        part = _dot(act.astype(BF16), wdb_ref[...])

        @pl.when(f == 0)
        def _():
            acc_ref[j] = part

        @pl.when(f > 0)
        def _():
            acc_ref[j] = acc_ref[j] + part

        @pl.when(f == N_FF - 1)
        def _():
            slot = st_ref[ST_DONE, s] % 2
            drain(slot)
            stage_ref[slot] = acc_ref[j] + bd_ref[0]

            def body(r, n):
                dst = dest_ref[r]

                @pl.when(dst >= 0)
                def _():
                    row_copy(slot, r, dst).start()
                return n + jnp.where(dst >= 0, 1, 0)
            inflight_ref[slot] = lax.fori_loop(0, MOE_BLOCK, body, 0, unroll=8)

    @pl.when(s == pl.num_programs(0) - 1)
    def _():
        drain(0)
        drain(1)


def _experts(steps, dest_row, xg, w_up, b_up, w_down, b_down, n_out_rows):
    n_steps = steps.shape[1]
    rows, wpk = xg.shape
    d = w_down.shape[2]
    e = w_up.shape[0]
    grid_spec = pltpu.PrefetchScalarGridSpec(
        num_scalar_prefetch=1,
        grid=(n_steps,),
        in_specs=[pl.BlockSpec((MOE_BLOCK,), lambda s, st: (st[ST_BLK, s],), memory_space=pltpu.SMEM),
                  pl.BlockSpec((MOE_BLOCK, wpk), lambda s, st: (st[ST_BLK, s], 0)),
                  pl.BlockSpec((1, d, FF_TILE), lambda s, st: (st[ST_E, s], 0, st[ST_F, s])),
                  pl.BlockSpec((1, d, FF_TILE), lambda s, st: (st[ST_E, s], 0, N_FF + st[ST_F, s])),
                  pl.BlockSpec((1, FF_TILE, d), lambda s, st: (st[ST_E, s], st[ST_F, s], 0)),
                  pl.BlockSpec((1, 1, FF_TILE), lambda s, st: (st[ST_E, s], 0, st[ST_F, s])),
                  pl.BlockSpec((1, 1, FF_TILE), lambda s, st: (st[ST_E, s], 0, N_FF + st[ST_F, s])),
                  pl.BlockSpec((1, 1, d), lambda s, st: (st[ST_E, s], 0, 0))],
        out_specs=pl.BlockSpec(memory_space=pl.ANY),
        scratch_shapes=[pltpu.VMEM((d, FF_TILE), BF16), pltpu.VMEM((d, FF_TILE), BF16), pltpu.VMEM((FF_TILE, d), BF16),
                        pltpu.VMEM((MOE_GROUP, MOE_BLOCK, d), F32),
                        pltpu.VMEM((2, MOE_BLOCK, d), F32),
                        pltpu.SemaphoreType.DMA((2,)),
                        pltpu.SMEM((2,), jnp.int32)],
    )
    return pl.pallas_call(
        _expert_kernel,
        grid_spec=grid_spec,
        out_shape=jax.ShapeDtypeStruct((n_out_rows, d), F32),
        compiler_params=_params(("arbitrary",)),
        name="moe_experts",
    )(steps, dest_row, xg, w_up, w_up, w_down, b_up.reshape(e, 1, -1), b_up.reshape(e, 1, -1), b_down.reshape(e, 1, d))


def _moe_plan(top_e, rank, counts, t):
    blk = MOE_BLOCK
    n_assign = TOP_K * t
    n_blocks = -(-n_assign // blk) + N_EXPERTS
    rows = n_blocks * blk
    nblk_e = (counts + blk - 1) // blk
    blk_end = jnp.cumsum(nblk_e)
    blk_start = blk_end - nblk_e
    used = blk_end[-1]
    e_ids = jnp.arange(N_EXPERTS, dtype=jnp.int32)
    start_of = jnp.sum(jnp.where(top_e[:, :, None] == e_ids, blk_start, 0), axis=-1)
    dest = start_of * blk + rank
    flat_slot_tok = (jnp.arange(TOP_K, dtype=jnp.int32)[:, None] * t + jnp.arange(t, dtype=jnp.int32)[None, :])
    dest_row = jnp.full((rows,), -1, jnp.int32).at[dest.reshape(-1)].set(flat_slot_tok.reshape(-1))
    row_tok = jnp.maximum(dest_row, 0) % t
    b = jnp.arange(n_blocks, dtype=jnp.int32)
    e_b = jnp.minimum(jnp.sum(blk_end[None, :] <= b[:, None], axis=1), N_EXPERTS - 1).astype(jnp.int32)
    lb = b - blk_start[e_b]
    j_b = lb % MOE_GROUP
    gsize = jnp.minimum(MOE_GROUP, nblk_e[e_b] - (lb - j_b))
    valid_b = b < used
    fidx = jnp.arange(N_FF, dtype=jnp.int32)
    step_of = N_FF * (b - j_b)[:, None] + fidx[None, :] * gsize[:, None] + j_b[:, None]
    n_steps = N_FF * n_blocks
    step_of = jnp.where(valid_b[:, None], step_of, n_steps)
    def scat(vals, fill):
        return jnp.full((n_steps,), fill, jnp.int32).at[step_of.reshape(-1)].set(
            jnp.broadcast_to(vals, (n_blocks, N_FF)).reshape(-1).astype(jnp.int32), mode="drop")
    st_valid = scat(jnp.ones((n_blocks, 1), jnp.int32), 0)
    st_blk = scat(b[:, None], -1)
    st_f = scat(fidx[None, :], -1)
    st_e = scat(e_b[:, None], -1)
    st_j = scat(j_b[:, None], 0)
    st_cast = scat((j_b == 0)[:, None], 0)
    n_valid_steps = N_FF * used
    last_idx = jnp.maximum(n_valid_steps - 1, 0)
    pad = jnp.arange(n_steps) >= n_valid_steps
    fix = lambda a: jnp.where(pad, a[last_idx], a)
    st_blk, st_f, st_e = fix(st_blk), fix(st_f), fix(st_e)
    finishing = (st_valid == 1) & (st_f == N_FF - 1)
    st_done = jnp.cumsum(finishing.astype(jnp.int32)) - finishing.astype(jnp.int32)
    steps = jnp.stack([st_blk, st_f, st_e, st_j, st_cast, st_valid, st_done]).astype(jnp.int32)
    return steps, row_tok, dest_row, rows


def _final_kernel(x1_ref, g2_ref, w_ref, y0_ref, y1_ref, y2_ref, y3_ref, op_ref, os_ref, *, n_prompt_tiles):
    w = w_ref[...]
    moe = (w[:, 0:1] * y0_ref[...] + w[:, 1:2] * y1_ref[...]) + (w[:, 2:3] * y2_ref[...] + w[:, 3:4] * y3_ref[...])
    y = x1_ref[...] + g2_ref[...] * moe
    i = pl.program_id(0)

    @pl.when(i < n_prompt_tiles)
    def _():
        op_ref[...] = y

    @pl.when(i >= n_prompt_tiles)
    def _():
        os_ref[...] = y


def _final(x1, mod_exp, top_w_t, ys, n_prompt_tiles):
    t, d = x1.shape
    rb = _mod_row_block(n_prompt_tiles)
    nt = t // TOK_TILE
    slot = lambda k: pl.BlockSpec((TOK_TILE, d), lambda i: (k * nt + i, 0))
    return pl.pallas_call(
        functools.partial(_final_kernel, n_prompt_tiles=n_prompt_tiles),
        grid=(nt,),
        in_specs=[pl.BlockSpec((TOK_TILE, d), lambda i: (i, 0)),
                  pl.BlockSpec((TOK_TILE, d), lambda i: (rb(i), 5)),
                  pl.BlockSpec((TOK_TILE, TOP_K), lambda i: (i, 0)),
                  slot(0), slot(1), slot(2), slot(3)],
        out_specs=_token_specs(d, n_prompt_tiles),
        out_shape=[jax.ShapeDtypeStruct((n_prompt_tiles * TOK_TILE, d), F32),
                   jax.ShapeDtypeStruct(((nt - n_prompt_tiles) * TOK_TILE, d), F32)],
        compiler_params=_params(("arbitrary",)),
        name="moe_combine",
    )(x1, mod_exp, top_w_t, ys, ys, ys, ys)


def _moe(h2p, logits_t, w_up, b_up, w_down, b_down):
    t = h2p.shape[0]
    top_e, top_w, rank, cnt = _route(logits_t)
    counts = cnt[:, 0].astype(jnp.int32)
    steps, row_tok, dest_row, rows = _moe_plan(top_e, rank, counts, t)
    xg = _dispatch(row_tok, h2p)
    ys = _experts(steps, dest_row, xg, w_up, b_up, w_down, b_down, TOP_K * t)
    return ys, top_w.T


PROJ_SIZES = (DN_CONV_CH, DN_HEADS * DN_DIM, DN_HEADS, DN_HEADS, SA_HEADS * SA_DIM, SA_KV_HEADS * SA_DIM,
              SA_KV_HEADS * SA_DIM, IDX_HEADS * IDX_DIM, IDX_DIM, IDX_HEADS, D_MODEL, D_MODEL)


def _split_w_in(w_in_t):
    ends = np.cumsum(PROJ_SIZES)
    seg = [w_in_t[int(e - s):int(e)] for s, e in zip(PROJ_SIZES, ends)]
    (dn_qkv, dn_z, dn_a, dn_b, sa_q, sa_k, sa_v, ix_q, ix_k, ix_w, gate_dn, gate_sa) = seg
    w_big = jnp.concatenate([dn_qkv, dn_z, sa_q, sa_k, sa_v, ix_q, gate_dn, gate_sa], axis=0).astype(BF16)
    pad = jnp.zeros((128 - (IDX_DIM + 2 * DN_HEADS + IDX_HEADS), w_in_t.shape[1]), w_in_t.dtype)
    w_small = jnp.concatenate([ix_k, dn_a, dn_b, ix_w, pad], axis=0)
    return w_big, w_small


def _pad_seq(a, bsz, t, t_pad, front=0):
    a = a.reshape(bsz, t, -1)
    return jnp.pad(a, ((0, 0), (front, t_pad - t - front), (0, 0))).reshape(bsz * t_pad, -1)


def kernel(x_prompt, x_sample, cache_k, cache_v, cache_kidx, state_conv, state_ssm, page_table, c_prompt, c_sample,
           rel_bias, w_ada, b_ada, g_norm1, w_in, conv_w, a_log, dt_bias, g_dn_out, g_q, g_k, w_o_dn, w_o_sa, w_out,
           g_norm2, w_router, b_router, w_up, b_up, w_down, b_down):
    assert w_ada.shape[0] == 1 and x_prompt.shape[0] == 1
    d = D_MODEL
    tp = x_prompt.shape[1]
    dbsz, dt = x_sample.shape[:2]
    ts = dbsz * dt
    assert dt == DEC_T and ts == TOK_TILE and tp % DSA_TQ == 0
    npt = tp // TOK_TILE
    x_p = x_prompt.reshape(tp, d)
    x_s = x_sample.reshape(ts, d)

    c_all = jnp.concatenate([c_prompt, c_sample], axis=0)
    c_all = jnp.pad(c_all, ((0, -c_all.shape[0] % 8), (0, 0)))
    mod = _adaln(c_all, w_ada[0], b_ada[0])
    mod_exp = jnp.concatenate([jnp.broadcast_to(mod[0:1], (TOK_TILE, mod.shape[1])),
                               jnp.repeat(mod[1:1 + dbsz], dt, axis=0)], axis=0)

    w_big_t, w_small_t = _split_w_in(jnp.swapaxes(w_in, 1, 2)[0])
    h, small = _norm_mod(x_p, x_s, g_norm1[0], mod_exp, w_small_t, npt)
    p = _matmul(h, w_big_t, F32, w_transposed=True)

    qn, ko_p, kb, vo_p, vt, qi, kio_p, kib, wit = _sa_prep(p, small, g_q[0], g_k[0], 0, tp, DSA_TQ)
    o_sa_p = _dsa_prompt(qn, kb, vt, qi, kib, wit, rel_bias)
    qn_s, ko_s, _, vo_s, _, qi_s, kio_s, _, wit_s = _sa_prep(p, small, g_q[0], g_k[0], tp, ts, TOK_TILE)
    n_pool = cache_k.shape[1]
    o_sa_s = _dsa_sample(qn_s, ko_s, vo_s, qi_s, kio_s, wit_s, cache_k.reshape(n_pool, *cache_k.shape[2:]),
                         cache_v.reshape(n_pool, *cache_v.shape[2:]),
                         jnp.swapaxes(cache_kidx, 2, 3).reshape(n_pool, IDX_DIM, PAGE), page_table, rel_bias)

    tm = 256
    xc, beta, g = _dn_prep(p, C_QKV, p, lambda i: (jnp.maximum(i * (tm // 8) - 1, 0), 0), small[:tp],
                           conv_w[0], a_log[0], dt_bias[0], tm=tm, n_valid=tm, zero_first_halo=True)
    chunk = 64
    s0_p = jnp.zeros((1,) + state_ssm.shape[2:], F32)
    o_dn_p, ssm_p = _dn_scan(xc, p, C_Z, beta, g, s0_p, g_dn_out[0], bsz=1, n_chunks=tp // chunk, c=chunk)
    qkv_s = p[tp:, C_QKV:C_QKV + DN_CONV_CH]
    halo_s = _pad_seq(state_conv[0].reshape(dbsz * (CONV_WIDTH - 1), -1), dbsz, CONV_WIDTH - 1, 8, front=8 - (CONV_WIDTH - 1))
    xc_s, beta_s, g_s = _dn_prep(_pad_seq(qkv_s, dbsz, dt, 8), 0, halo_s, lambda i: (i, 0), _pad_seq(small[tp:], dbsz, dt, 8),
                                 conv_w[0], a_log[0], dt_bias[0], tm=8, n_valid=dt, zero_first_halo=False)
    z_s = _pad_seq(p[tp:, C_Z:C_Z + DN_HEADS * DN_DIM], dbsz, dt, 8)
    o_dn_s, ssm_s = _dn_scan(xc_s, z_s, 0, beta_s, g_s, state_ssm[0], g_dn_out[0], bsz=dbsz, n_chunks=1, c=8)
    o_dn_s = o_dn_s.reshape(dbsz, 8, -1)[:, :dt].reshape(ts, -1)
    conv_p = p[tp - (CONV_WIDTH - 1):tp, C_QKV:C_QKV + DN_CONV_CH]
    conv_s = jnp.concatenate([state_conv[0], qkv_s.reshape(dbsz, dt, -1)], axis=1)[:, -(CONV_WIDTH - 1):]

    o_dn = jnp.concatenate([o_dn_p, o_dn_s], axis=0)
    o_sa = jnp.concatenate([o_sa_p, o_sa_s.astype(BF16)], axis=0)
    merged = _merge(o_dn, o_sa, p, w_o_dn[0].astype(BF16), w_o_sa[0].astype(BF16))
    m2 = _matmul(merged, w_out[0].astype(BF16), F32)
    x1, h2p, logits_t = _post_attn(x_p, x_s, m2, mod_exp, g_norm2[0], jnp.swapaxes(w_router, 1, 2)[0], b_router[0], npt)
    ys, top_w_t = _moe(h2p, logits_t, w_up[0], b_up[0], w_down[0], b_down[0])
    y_p, y_s = _final(x1, mod_exp, top_w_t, ys, npt)

    kv = (SA_KV_HEADS, SA_DIM)
    return (y_p.reshape(x_prompt.shape), y_s.reshape(x_sample.shape),
            ko_p.reshape(1, 1, tp, *kv), vo_p.reshape(1, 1, tp, *kv), kio_p.reshape(1, 1, tp, IDX_DIM),
            conv_p.reshape(1, 1, CONV_WIDTH - 1, DN_CONV_CH), ssm_p.reshape(1, *ssm_p.shape),
            ko_s.reshape(1, dbsz, dt, *kv), vo_s.reshape(1, dbsz, dt, *kv), kio_s.reshape(1, dbsz, dt, IDX_DIM),
            conv_s.reshape(1, dbsz, CONV_WIDTH - 1, DN_CONV_CH), ssm_s.reshape(1, *ssm_s.shape))
```

```python
import functools
import math

import jax
import jax.numpy as jnp
import numpy as np
from jax import lax
from jax.experimental import pallas as pl
from jax.experimental.pallas import tpu as pltpu

D_MODEL = 2048
DN_HEADS = 16
DN_DIM = 128
CONV_WIDTH = 4
DN_CONV_CH = 3 * DN_HEADS * DN_DIM
SA_HEADS = 16
SA_KV_HEADS = 4
SA_GROUP = SA_HEADS // SA_KV_HEADS
SA_DIM = 128
IDX_HEADS = 16
IDX_DIM = 64
TOPK = 256
REL_BUCKETS = 32
REL_MAX_DIST = 128
N_EXPERTS = 32
TOP_K = 4
D_FF = 2048
SWIGLU_LIMIT = 7.0
SWIGLU_ALPHA = 1.702
MOE_BLOCK = 256
PAGE = 128
EPS = 1e-6

F32 = jnp.float32
BF16 = jnp.bfloat16
HI = lax.Precision.HIGHEST

VMEM_LIMIT = 56 * 1024 * 1024
TOK_TILE = 128

C_QKV, C_Z, C_SQ, C_SK, C_SV, C_IQ, C_GD, C_GS, C_BIG = 0, 6144, 8192, 10240, 10752, 11264, 12288, 14336, 16384
S_IK, S_A, S_B, S_IW = 0, 64, 80, 96


def _params(sem, vmem=VMEM_LIMIT):
    return pltpu.CompilerParams(dimension_semantics=sem, vmem_limit_bytes=vmem)


def _dot(a, b):
    return jnp.dot(a, b, preferred_element_type=F32)


def _dot_nt(a, b):
    return lax.dot_general(a, b, (((1,), (1,)), ((), ())), preferred_element_type=F32)


def _split_bf16(a):
    hi = a.astype(BF16)
    lo = (a - hi.astype(F32)).astype(BF16)
    return hi, lo


def _dot3(a, b):
    ah, al = _split_bf16(a)
    bh, bl = _split_bf16(b)
    return _dot(ah, bh) + (_dot(ah, bl) + _dot(al, bh))


def _ada_kernel(c_ref, w_ref, b_ref, o_ref):
    o_ref[...] = jnp.dot(c_ref[...], w_ref[...], precision=HI, preferred_element_type=F32) + b_ref[...]


def _adaln(c_all, w_ada, b_ada):
    r, d = c_all.shape
    n = w_ada.shape[1]
    tn = 1024
    return pl.pallas_call(
        _ada_kernel,
        grid=(n // tn,),
        in_specs=[pl.BlockSpec((r, d), lambda j: (0, 0)),
                  pl.BlockSpec((d, tn), lambda j: (0, j)),
                  pl.BlockSpec((1, tn), lambda j: (0, j))],
        out_specs=pl.BlockSpec((r, tn), lambda j: (0, j)),
        out_shape=jax.ShapeDtypeStruct((r, n), F32),
        compiler_params=_params(("parallel",)),
        name="adaln",
    )(c_all, w_ada, b_ada.reshape(1, n))


def _norm_mod_kernel(xp_ref, xs_ref, g_ref, sh_ref, sc_ref, ws_ref, h_ref, s_ref, *, n_prompt_tiles):
    x = jnp.where(pl.program_id(0) < n_prompt_tiles, xp_ref[...], xs_ref[...])
    y = x * lax.rsqrt(jnp.mean(x * x, axis=-1, keepdims=True) + EPS) * g_ref[...]
    h = y * (1.0 + sc_ref[...]) + sh_ref[...]
    h_ref[...] = h.astype(BF16)
    s_ref[...] = lax.dot_general(h, ws_ref[...], (((1,), (1,)), ((), ())), precision=HI, preferred_element_type=F32)


def _mod_row_block(n_prompt_tiles):
    return lambda i: jnp.maximum(i - n_prompt_tiles + 1, 0)


def _token_specs(d, n_prompt_tiles):
    last = n_prompt_tiles - 1
    return [pl.BlockSpec((TOK_TILE, d), lambda i: (jnp.minimum(i, last), 0)),
            pl.BlockSpec((TOK_TILE, d), lambda i: (jnp.maximum(i - n_prompt_tiles, 0), 0))]


def _norm_mod(x_p, x_s, g, mod_exp, w_small_t, n_prompt_tiles):
    d = x_p.shape[1]
    t = x_p.shape[0] + x_s.shape[0]
    rb = _mod_row_block(n_prompt_tiles)
    return pl.pallas_call(
        functools.partial(_norm_mod_kernel, n_prompt_tiles=n_prompt_tiles),
        grid=(t // TOK_TILE,),
        in_specs=_token_specs(d, n_prompt_tiles) + [
            pl.BlockSpec((1, d), lambda i: (0, 0)),
            pl.BlockSpec((TOK_TILE, d), lambda i: (rb(i), 0)),
            pl.BlockSpec((TOK_TILE, d), lambda i: (rb(i), 1)),
            pl.BlockSpec((128, d), lambda i: (0, 0))],
        out_specs=[pl.BlockSpec((TOK_TILE, d), lambda i: (i, 0)),
                   pl.BlockSpec((TOK_TILE, 128), lambda i: (i, 0))],
        out_shape=[jax.ShapeDtypeStruct((t, d), BF16), jax.ShapeDtypeStruct((t, 128), F32)],
        compiler_params=_params(("parallel",)),
        name="norm_mod",
    )(x_p, x_s, g.reshape(1, d), mod_exp, mod_exp, w_small_t)


def _mm_kernel(x_ref, w_ref, o_ref):
    o_ref[...] = _dot(x_ref[...], w_ref[...]).astype(o_ref.dtype)


def _mm_nt_kernel(x_ref, wt_ref, o_ref):
    o_ref[...] = _dot_nt(x_ref[...], wt_ref[...]).astype(o_ref.dtype)


def _row_tile(m, cap=1024):
    for tm in (1024, 896, 832, 768, 640, 512, 384, 256, 128):
        if tm <= cap and m % tm == 0:
            return tm
    raise ValueError(m)


def _matmul(x, w, out_dtype, tn=512, w_transposed=False):
    m, k = x.shape
    n = w.shape[0] if w_transposed else w.shape[1]
    tm = _row_tile(m)
    w_spec = pl.BlockSpec((tn, k), lambda j, i: (j, 0)) if w_transposed else pl.BlockSpec((k, tn), lambda j, i: (0, j))
    return pl.pallas_call(
        _mm_nt_kernel if w_transposed else _mm_kernel,
        grid=(n // tn, m // tm),
        in_specs=[pl.BlockSpec((tm, k), lambda j, i: (i, 0)), w_spec],
        out_specs=pl.BlockSpec((tm, tn), lambda j, i: (i, j)),
        out_shape=jax.ShapeDtypeStruct((m, n), out_dtype),
        compiler_params=_params(("parallel", "parallel")),
        name="matmul",
    )(x, w)


def _sa_prep_kernel(q_ref, k_ref, v_ref, iq_ref, s_ref, gq_ref, gk_ref,
                    qn_ref, ko_ref, kb_ref, vo_ref, vt_ref, qi_ref, kio_ref, kib_ref, wit_ref):
    gq = gq_ref[...]
    gk = gk_ref[...]
    for h in range(SA_HEADS):
        x = q_ref[:, h * SA_DIM:(h + 1) * SA_DIM]
        y = x * lax.rsqrt(jnp.mean(x * x, axis=-1, keepdims=True) + EPS) * gq
        qn_ref[h] = (y * (SA_DIM ** -0.5 * LOG2E)).astype(BF16)
    for g in range(SA_KV_HEADS):
        x = k_ref[:, g * SA_DIM:(g + 1) * SA_DIM]
        y = x * lax.rsqrt(jnp.mean(x * x, axis=-1, keepdims=True) + EPS) * gk
        ko_ref[:, g * SA_DIM:(g + 1) * SA_DIM] = y
        kb_ref[g] = y.astype(BF16)
        v = v_ref[:, g * SA_DIM:(g + 1) * SA_DIM]
        vt_ref[g, 0] = v.T.astype(BF16)
    vo_ref[...] = v_ref[...]
    for h in range(IDX_HEADS):
        qi_ref[h] = iq_ref[:, h * IDX_DIM:(h + 1) * IDX_DIM].astype(BF16)
    s = s_ref[...]
    ki = s[:, S_IK:S_IK + IDX_DIM]
    kio_ref[...] = ki
    kib_ref[...] = ki.astype(BF16)
    wit_ref[...] = s.T[S_IW:S_IW + IDX_HEADS, :] * ((IDX_HEADS ** -0.5) * (IDX_DIM ** -0.5))


def _sa_prep(p, small, g_q, g_k, row0, t, tm):
    nt = t // tm
    rb = row0 // tm
    col = lambda c0, w: (lambda i: (i + rb, c0 // w))
    out_shape = [
        jax.ShapeDtypeStruct((SA_HEADS, t, SA_DIM), BF16),
        jax.ShapeDtypeStruct((t, SA_KV_HEADS * SA_DIM), F32),
        jax.ShapeDtypeStruct((SA_KV_HEADS, t, SA_DIM), BF16),
        jax.ShapeDtypeStruct((t, SA_KV_HEADS * SA_DIM), F32),
        jax.ShapeDtypeStruct((SA_KV_HEADS, nt, SA_DIM, tm), BF16),
        jax.ShapeDtypeStruct((IDX_HEADS, t, IDX_DIM), BF16),
        jax.ShapeDtypeStruct((t, IDX_DIM), F32),
        jax.ShapeDtypeStruct((t, IDX_DIM), BF16),
        jax.ShapeDtypeStruct((IDX_HEADS, t), F32),
    ]
    out_specs = [
        pl.BlockSpec((SA_HEADS, tm, SA_DIM), lambda i: (0, i, 0)),
        pl.BlockSpec((tm, 512), lambda i: (i, 0)),
        pl.BlockSpec((SA_KV_HEADS, tm, SA_DIM), lambda i: (0, i, 0)),
        pl.BlockSpec((tm, 512), lambda i: (i, 0)),
        pl.BlockSpec((SA_KV_HEADS, 1, SA_DIM, tm), lambda i: (0, i, 0, 0)),
        pl.BlockSpec((IDX_HEADS, tm, IDX_DIM), lambda i: (0, i, 0)),
        pl.BlockSpec((tm, IDX_DIM), lambda i: (i, 0)),
        pl.BlockSpec((tm, IDX_DIM), lambda i: (i, 0)),
        pl.BlockSpec((IDX_HEADS, tm), lambda i: (0, i)),
    ]
    return pl.pallas_call(
        _sa_prep_kernel,
        grid=(nt,),
        in_specs=[pl.BlockSpec((tm, 2048), col(C_SQ, 2048)),
                  pl.BlockSpec((tm, 512), col(C_SK, 512)),
                  pl.BlockSpec((tm, 512), col(C_SV, 512)),
                  pl.BlockSpec((tm, 1024), col(C_IQ, 1024)),
                  pl.BlockSpec((tm, 128), lambda i: (i + rb, 0)),
                  pl.BlockSpec((1, SA_DIM), lambda i: (0, 0)),
                  pl.BlockSpec((1, SA_DIM), lambda i: (0, 0))],
        out_specs=out_specs,
        out_shape=out_shape,
        compiler_params=_params(("parallel",)),
        name="sa_prep",
    )(p, p, p, p, small, g_q.reshape(1, SA_DIM), g_k.reshape(1, SA_DIM))


DSA_TQ = 256
BISECT_ITERS = 40
LOG2E = 1.4426950408889634


def _t5_bucket(dist):
    n = jnp.maximum(dist, 0)
    exact = REL_BUCKETS // 2
    log_ratio = jnp.log(jnp.maximum(n, exact).astype(F32) / exact) / math.log(REL_MAX_DIST / exact)
    large = jnp.minimum(exact + (log_ratio * (REL_BUCKETS - exact)).astype(jnp.int32), REL_BUCKETS - 1)
    return jnp.where(n < exact, n, large)


def _topk_threshold(sc_ref, n_tiles, tk, tq):
    inf = jnp.float32(jnp.inf)

    def tile(c):
        return sc_ref[pl.ds(pl.multiple_of(c * tk, tk), tk), :].reshape(tk // 8, 8, tq)

    def mm_body(c, carry):
        mn, mx = carry
        blk = tile(c)
        return (jnp.minimum(mn, jnp.where(blk == -inf, inf, blk).min(axis=0)), jnp.maximum(mx, blk.max(axis=0)))

    mn8, mx8 = lax.fori_loop(0, n_tiles, mm_body, (jnp.full((8, tq), inf, F32), jnp.full((8, tq), -inf, F32)))
    lo = mn8.min(axis=0, keepdims=True)
    hi = mx8.max(axis=0, keepdims=True)

    def count_ge(v):
        def cnt_body(c, acc):
            return acc + jnp.where(tile(c) >= v, 1.0, 0.0).sum(axis=0)
        return lax.fori_loop(0, n_tiles, cnt_body, jnp.zeros((8, tq), F32)).sum(axis=0, keepdims=True)

    def unfinished(cnt_lo):
        return (jnp.max(cnt_lo) > float(TOPK)).astype(jnp.int32)

    def cond(carry):
        it, go = carry[0], carry[1]
        return (it < BISECT_ITERS) & (go > 0)

    def body(carry):
        it, _, lo, hi, cnt_lo = carry
        mid = 0.5 * (lo + hi)
        cnt = count_ge(mid)
        ok = cnt >= float(TOPK)
        lo = jnp.where(ok, mid, lo)
        cnt_lo = jnp.where(ok, cnt, cnt_lo)
        return it + 1, unfinished(cnt_lo), lo, jnp.where(ok, hi, mid), cnt_lo

    cnt_lo = count_ge(lo)
    carry = lax.while_loop(cond, body, (jnp.int32(0), unfinished(cnt_lo), lo, hi, cnt_lo))
    return carry[2]


def _dsa_prompt_kernel(rb_ref, qi_ref, wit_ref, ki_ref, q_ref, k_ref, vt_ref,
                       o_ref, sc_ref, bias_ref, acc_ref, m_ref, l_ref):
    tq = tk = DSA_TQ
    gw = SA_GROUP * tq
    qb = pl.program_id(0)
    ninf = jnp.float32(-jnp.inf)

    @pl.when(qb == 0)
    def _():
        def bias_rows(rc, carry):
            r0 = pl.multiple_of(rc * 8, 8)
            dist = (lax.broadcasted_iota(jnp.int32, (8, tq), 1) + tk) - (r0 + lax.broadcasted_iota(jnp.int32, (8, tq), 0))
            bucket = _t5_bucket(dist)
            hit = [bucket == i for i in range(REL_BUCKETS)]
            for h in range(SA_HEADS):
                val = jnp.zeros((8, tq), F32)
                for i in range(REL_BUCKETS):
                    val = jnp.where(hit[i], rb_ref[i, h], val)
                j = h % SA_GROUP
                bias_ref[h // SA_GROUP, pl.ds(r0, 8), j * tq:(j + 1) * tq] = (val - rb_ref[REL_BUCKETS - 1, h]) * LOG2E
            return carry

        lax.fori_loop(0, 2 * tk // 8, bias_rows, 0)

    def idx_body(kt, carry):
        r0 = pl.multiple_of(kt * tk, tk)
        ki_t = ki_ref[pl.ds(r0, tk), :]
        for h in range(IDX_HEADS):
            part = wit_ref[h:h + 1, :] * jnp.maximum(_dot_nt(ki_t, qi_ref[h]), 0.0)
            if h == 0:
                sc_ref[pl.ds(r0, tk), :] = part
            else:
                sc_ref[pl.ds(r0, tk), :] += part
        return carry

    lax.fori_loop(0, qb + 1, idx_body, 0)
    d0 = pl.multiple_of(qb * tk, tk)
    causal = lax.broadcasted_iota(jnp.int32, (tk, tq), 0) <= lax.broadcasted_iota(jnp.int32, (tk, tq), 1)
    sc_ref[pl.ds(d0, tk), :] = jnp.where(causal, sc_ref[pl.ds(d0, tk), :], ninf)

    thr = _topk_threshold(sc_ref, qb + 1, tk, tq)

    def mask_body(c, carry):
        r0 = pl.multiple_of(c * tk, tk)
        sc_ref[pl.ds(r0, tk), :] = jnp.where(sc_ref[pl.ds(r0, tk), :] >= thr, 0.0, ninf)
        return carry

    lax.fori_loop(0, qb + 1, mask_body, 0)

    m_ref[...] = jnp.full(m_ref.shape, ninf, F32)
    l_ref[...] = jnp.zeros(l_ref.shape, F32)
    acc_ref[...] = jnp.zeros(acc_ref.shape, F32)

    def tile(kt, bias_row0):
        r0 = pl.multiple_of(kt * tk, tk)
        madd1 = sc_ref[pl.ds(r0, tk), :]
        madd = jnp.concatenate([madd1] * SA_GROUP, axis=1)
        for g in range(SA_KV_HEADS):
            qg = q_ref[g * SA_GROUP:(g + 1) * SA_GROUP].reshape(gw, SA_DIM)
            s = _dot_nt(k_ref[g, pl.ds(r0, tk), :], qg) + madd
            if bias_row0 is not None:
                s = s + bias_ref[g, bias_row0:bias_row0 + tk, :]
            m_old = m_ref[g]
            m_new = jnp.maximum(m_old, s.max(axis=0, keepdims=True))
            m_safe = jnp.where(m_new == ninf, 0.0, m_new)
            p = jnp.exp2(s - m_safe)
            alpha = jnp.exp2(m_old - m_safe)
            l_ref[g] = l_ref[g] * alpha + p.sum(axis=0, keepdims=True)
            acc_ref[g] = acc_ref[g] * alpha + _dot(vt_ref[g, kt], p.astype(BF16))
            m_ref[g] = m_new

    def far_body(kt, carry):
        tile(kt, None)
        return carry

    lax.fori_loop(0, jnp.maximum(qb - 1, 0), far_body, 0)

    @pl.when(qb > 0)
    def _():
        tile(qb - 1, 0)

    tile(qb, tk)

    for h in range(SA_HEADS):
        g, j = divmod(h, SA_GROUP)
        out_t = acc_ref[g, :, j * tq:(j + 1) * tq] / l_ref[g, :, j * tq:(j + 1) * tq]
        o_ref[:, h * SA_DIM:(h + 1) * SA_DIM] = out_t.T.astype(BF16)


def _resident(shape):
    nd = len(shape)
    return pl.BlockSpec(shape, lambda i: (0,) * nd, pipeline_mode=pl.Buffered(1))


def _dsa_prompt(qn, kb, vt, qi, kib, wit, rel_bias):
    t = qn.shape[1]
    tq = DSA_TQ
    assert t % tq == 0 and tq >= REL_MAX_DIST
    gw = SA_GROUP * tq
    return pl.pallas_call(
        _dsa_prompt_kernel,
        grid=(t // tq,),
        in_specs=[pl.BlockSpec(memory_space=pltpu.SMEM),
                  pl.BlockSpec((IDX_HEADS, tq, IDX_DIM), lambda i: (0, i, 0)),
                  pl.BlockSpec((IDX_HEADS, tq), lambda i: (0, i)),
                  _resident(kib.shape),
                  pl.BlockSpec((SA_HEADS, tq, SA_DIM), lambda i: (0, i, 0)),
                  _resident(kb.shape),
                  _resident(vt.shape)],
        out_specs=pl.BlockSpec((tq, SA_HEADS * SA_DIM), lambda i: (i, 0)),
        out_shape=jax.ShapeDtypeStruct((t, SA_HEADS * SA_DIM), BF16),
        scratch_shapes=[pltpu.VMEM((t, tq), F32),
                        pltpu.VMEM((SA_KV_HEADS, 2 * tq, gw), F32),
                        pltpu.VMEM((SA_KV_HEADS, SA_DIM, gw), F32),
                        pltpu.VMEM((SA_KV_HEADS, 1, gw), F32),
                        pltpu.VMEM((SA_KV_HEADS, 1, gw), F32)],
        compiler_params=_params(("arbitrary",)),
        name="dsa_prompt",
    )(rel_bias.astype(F32), qi, wit, kib, qn, kb, vt)


def _dn_prep_kernel(x_ref, halo_ref, cw_ref, s_ref, alog_ref, dtb_ref, o_ref, beta_ref, g_ref,
                    *, tm, n_valid, zero_first_halo):
    x = x_ref[...]
    halo = halo_ref[...]
    if zero_first_halo:
        halo = jnp.where(pl.program_id(0) == 0, 0.0, halo)
    xcat = jnp.concatenate([halo, x], axis=0)
    cw = cw_ref[...]
    conv = x * cw[CONV_WIDTH - 1:CONV_WIDTH, :]
    for j in range(CONV_WIDTH - 1):
        lo = 8 - (CONV_WIDTH - 1) + j
        conv = conv + xcat[lo:lo + tm, :] * cw[j:j + 1, :]
    act = conv * jax.nn.sigmoid(conv)
    valid = None
    if n_valid < tm:
        valid = lax.broadcasted_iota(jnp.int32, (tm, 1), 0) < n_valid
    nqk = 2 * DN_HEADS
    for h in range(3 * DN_HEADS):
        a = act[:, h * DN_DIM:(h + 1) * DN_DIM]
        if h < nqk:
            a = a * lax.rsqrt(jnp.sum(a * a, axis=-1, keepdims=True) + EPS)
            if h < DN_HEADS:
                a = a * (DN_DIM ** -0.5)
        if valid is not None:
            a = jnp.where(valid, a, 0.0)
        o_ref[:, h * DN_DIM:(h + 1) * DN_DIM] = a
    s = s_ref[...]
    a_raw = s[:, S_A:S_A + DN_HEADS] + dtb_ref[...]
    softplus = jnp.maximum(a_raw, 0.0) + jnp.log1p(jnp.exp(-jnp.abs(a_raw)))
    g = -jnp.exp(alog_ref[...]) * softplus
    beta = jax.nn.sigmoid(s[:, S_B:S_B + DN_HEADS])
    if valid is not None:
        g = jnp.where(valid, g, 0.0)
        beta = jnp.where(valid, beta, 0.0)
    g_ref[...] = g
    beta_ref[...] = beta


def _dn_prep(x, x_col0, halo, halo_index, small, conv_w, a_log, dt_bias, *, tm, n_valid, zero_first_halo):
    t = small.shape[0]
    c = DN_CONV_CH
    cb = x_col0 // c
    kern = functools.partial(_dn_prep_kernel, tm=tm, n_valid=n_valid, zero_first_halo=zero_first_halo)
    return pl.pallas_call(
        kern,
        grid=(t // tm,),
        in_specs=[pl.BlockSpec((tm, c), lambda i: (i, cb)),
                  pl.BlockSpec((8, c), halo_index),
                  pl.BlockSpec((CONV_WIDTH, c), lambda i: (0, 0)),
                  pl.BlockSpec((tm, 128), lambda i: (i, 0)),
                  pl.BlockSpec((1, DN_HEADS), lambda i: (0, 0)),
                  pl.BlockSpec((1, DN_HEADS), lambda i: (0, 0))],
        out_specs=[pl.BlockSpec((tm, c), lambda i: (i, 0)),
                   pl.BlockSpec((tm, DN_HEADS), lambda i: (i, 0)),
                   pl.BlockSpec((tm, DN_HEADS), lambda i: (i, 0))],
        out_shape=[jax.ShapeDtypeStruct((t, c), F32),
                   jax.ShapeDtypeStruct((t, DN_HEADS), F32),
                   jax.ShapeDtypeStruct((t, DN_HEADS), F32)],
        compiler_params=_params(("parallel",)),
        name="dn_prep",
    )(x, halo, conv_w, small, a_log.reshape(1, DN_HEADS), dt_bias.reshape(1, DN_HEADS))


DN_SOLVE_BLOCK = 16


def _dn_scan_kernel(x_ref, z_ref, beta_ref, g_ref, gt_ref, s0_ref, gout_ref, o_ref, sfin_ref,
                    s_ref, a_ref, t_ref, n_ref, rhs_ref, attn_ref, vnew_ref, *, c):
    n = pl.program_id(1)
    heads = range(DN_HEADS)

    @pl.when(n == 0)
    def _():
        s_ref[...] = s0_ref[0]

    row = lax.broadcasted_iota(jnp.int32, (c, c), 0)
    col = lax.broadcasted_iota(jnp.int32, (c, c), 1)
    incl = row >= col
    strict = row > col
    eye = (row == col).astype(F32)
    sb = min(c, DN_SOLVE_BLOCK)
    same_blk = (row // sb) == (col // sb)
    g_col = g_ref[...]
    g_row = gt_ref[:, 0, 0, :]
    gc_col = jnp.dot(incl.astype(F32), g_col, precision=HI, preferred_element_type=F32)
    gc_row = jnp.dot(g_row, (row <= col).astype(F32), precision=HI, preferred_element_type=F32)
    beta = beta_ref[...]
    gout = gout_ref[...]
    qs = lambda h: x_ref[:, h * DN_DIM:(h + 1) * DN_DIM]
    ks = lambda h: x_ref[:, (DN_HEADS + h) * DN_DIM:(DN_HEADS + h + 1) * DN_DIM]
    vs = lambda h: x_ref[:, (2 * DN_HEADS + h) * DN_DIM:(2 * DN_HEADS + h + 1) * DN_DIM]

    for h in heads:
        q, k, v = qs(h), ks(h), vs(h)
        gcc = gc_col[:, h:h + 1]
        gcr = gc_row[h:h + 1, :]
        bcol = beta[:, h:h + 1]
        decay = jnp.where(incl, jnp.exp(jnp.where(incl, gcc - gcr, 0.0)), 0.0)
        kb = k * bcol
        kt = k.T
        a = jnp.where(strict, -(_dot3(kb, kt) * decay), 0.0)
        a_diag = jnp.where(same_blk, a, 0.0)
        a_ref[h] = a_diag
        t_ref[h] = eye + a_diag
        n_ref[h] = a - a_diag
        rhs_ref[h] = jnp.concatenate([v * bcol, kb * jnp.exp(gcc)], axis=1)
        attn_ref[h] = jnp.where(incl, _dot3(q, kt) * decay, 0.0)

    for _ in range(int(math.log2(sb)) - 1):
        for h in heads:
            p = a_ref[h]
            p2 = _dot3(p, p)
            a_ref[h] = p2
            t = t_ref[h]
            t_ref[h] = t + _dot3(t, p2)

    for i in range(c // sb):
        rows = slice(i * sb, (i + 1) * sb)
        if i > 0:
            for h in heads:
                rhs_ref[h, rows, :] = rhs_ref[h, rows, :] + _dot3(n_ref[h, rows, :], rhs_ref[h])
        for h in heads:
            rhs_ref[h, rows, :] = _dot3(t_ref[h, rows, :], rhs_ref[h])

    for h in heads:
        vnew_ref[h] = rhs_ref[h, :, :DN_DIM] - _dot3(rhs_ref[h, :, DN_DIM:], s_ref[h])

    for h in heads:
        q, k = qs(h), ks(h)
        gcc = gc_col[:, h:h + 1]
        state = s_ref[h]
        v_new = vnew_ref[h]
        out = _dot3(q * jnp.exp(gcc), state) + _dot3(attn_ref[h], v_new)
        g_last = gcc[c - 1:c, :]
        k_dec = k * jnp.exp(g_last - gcc)
        s_ref[h] = state * jnp.exp(g_last) + _dot3(k_dec.T, v_new)
        y = out * lax.rsqrt(jnp.mean(out * out, axis=-1, keepdims=True) + EPS) * gout
        z = z_ref[:, h * DN_DIM:(h + 1) * DN_DIM]
        o_ref[:, h * DN_DIM:(h + 1) * DN_DIM] = (y * (z * jax.nn.sigmoid(z))).astype(o_ref.dtype)

    @pl.when(n == pl.num_programs(1) - 1)
    def _():
        sfin_ref[0] = s_ref[...]


def _dn_scan(xc, z, z_col0, beta, g, s0, g_out, *, bsz, n_chunks, c):
    t = xc.shape[0]
    hdim = DN_HEADS * DN_DIM
    zb = z_col0 // hdim
    gt = g.T.reshape(DN_HEADS, bsz * n_chunks, 1, c)
    kern = functools.partial(_dn_scan_kernel, c=c)
    return pl.pallas_call(
        kern,
        grid=(bsz, n_chunks),
        in_specs=[pl.BlockSpec((c, DN_CONV_CH), lambda b, n: (b * n_chunks + n, 0)),
                  pl.BlockSpec((c, hdim), lambda b, n: (b * n_chunks + n, zb)),
                  pl.BlockSpec((c, DN_HEADS), lambda b, n: (b * n_chunks + n, 0)),
                  pl.BlockSpec((c, DN_HEADS), lambda b, n: (b * n_chunks + n, 0)),
                  pl.BlockSpec((DN_HEADS, 1, 1, c), lambda b, n: (0, b * n_chunks + n, 0, 0)),
                  pl.BlockSpec((1, DN_HEADS, DN_DIM, DN_DIM), lambda b, n: (b, 0, 0, 0)),
                  pl.BlockSpec((1, DN_DIM), lambda b, n: (0, 0))],
        out_specs=[pl.BlockSpec((c, hdim), lambda b, n: (b * n_chunks + n, 0)),
                   pl.BlockSpec((1, DN_HEADS, DN_DIM, DN_DIM), lambda b, n: (b, 0, 0, 0))],
        out_shape=[jax.ShapeDtypeStruct((t, hdim), BF16),
                   jax.ShapeDtypeStruct((bsz, DN_HEADS, DN_DIM, DN_DIM), F32)],
        scratch_shapes=[pltpu.VMEM((DN_HEADS, DN_DIM, DN_DIM), F32),
                        pltpu.VMEM((DN_HEADS, c, c), F32),
                        pltpu.VMEM((DN_HEADS, c, c), F32),
                        pltpu.VMEM((DN_HEADS, c, c), F32),
                        pltpu.VMEM((DN_HEADS, c, 2 * DN_DIM), F32),
                        pltpu.VMEM((DN_HEADS, c, c), F32),
                        pltpu.VMEM((DN_HEADS, c, DN_DIM), F32)],
        compiler_params=_params(("parallel", "arbitrary")),
        name="dn_scan",
    )(xc, z, beta, g, gt, s0, g_out.reshape(1, DN_DIM))


S1_PAGES = 8
S3_PAGES = 8
DEC_T = 4


def _dsa_s_scores_kernel(pt_ref, qi_ref, wsel_ref, kin_ref, *refs):
    pages = refs[:S1_PAGES]
    sc_ref, scn_ref = refs[S1_PAGES:]
    qi = qi_ref[0]
    wsel = wsel_ref[0]

    def score(keys_t):
        r = jnp.maximum(_dot(qi, keys_t.astype(BF16)), 0.0)
        return jnp.dot(wsel, r, precision=HI, preferred_element_type=F32)

    for i in range(S1_PAGES):
        sc_ref[0, :, i * PAGE:(i + 1) * PAGE] = score(pages[i][0])
    sn = score(kin_ref[0])
    t_idx = lax.broadcasted_iota(jnp.int32, sn.shape, 0)
    c_idx = lax.broadcasted_iota(jnp.int32, sn.shape, 1)
    scn_ref[0] = jnp.where((c_idx <= t_idx) & (c_idx < DEC_T), sn, -jnp.inf)


def _dsa_s_scores(page_table, qi_s, wsel, ki_new_t, cache_kidx_t):
    dbsz, n_pages = page_table.shape
    past = n_pages * PAGE
    steps = n_pages // S1_PAGES
    page_spec = lambda i: pl.BlockSpec((1, IDX_DIM, PAGE), lambda b, j, pt: (pt[b * n_pages + j * S1_PAGES + i], 0, 0))
    grid_spec = pltpu.PrefetchScalarGridSpec(
        num_scalar_prefetch=1,
        grid=(dbsz, steps),
        in_specs=[pl.BlockSpec((1,) + qi_s.shape[1:], lambda b, j, pt: (b, 0, 0)),
                  pl.BlockSpec((1,) + wsel.shape[1:], lambda b, j, pt: (b, 0, 0)),
                  pl.BlockSpec((1, IDX_DIM, PAGE), lambda b, j, pt: (b, 0, 0))]
                 + [page_spec(i) for i in range(S1_PAGES)],
        out_specs=[pl.BlockSpec((1, 8, S1_PAGES * PAGE), lambda b, j, pt: (b, 0, j)),
                   pl.BlockSpec((1, 8, PAGE), lambda b, j, pt: (b, 0, 0))],
    )
    return pl.pallas_call(
        _dsa_s_scores_kernel,
        grid_spec=grid_spec,
        out_shape=[jax.ShapeDtypeStruct((dbsz, 8, past), F32), jax.ShapeDtypeStruct((dbsz, 8, PAGE), F32)],
        compiler_params=_params(("parallel", "arbitrary")),
        name="dsa_sample_scores",
    )(page_table.reshape(-1), qi_s, wsel, ki_new_t, *([cache_kidx_t] * S1_PAGES))


def _dsa_s_mask_kernel(sc_ref, scn_ref, m_ref, mn_ref):
    inf = jnp.float32(jnp.inf)

    def row_min(s):
        return jnp.where(s == -inf, inf, s).min(axis=1, keepdims=True)

    lo = jnp.minimum(row_min(sc_ref[...]), row_min(scn_ref[...]))
    hi = jnp.maximum(sc_ref[...].max(axis=1, keepdims=True), scn_ref[...].max(axis=1, keepdims=True))

    def count(s, mid):
        return jnp.where(s >= mid, 1.0, 0.0).sum(axis=1, keepdims=True)

    def bis_body(_, carry):
        lo, hi = carry
        mid = 0.5 * (lo + hi)
        ok = (count(sc_ref[...], mid) + count(scn_ref[...], mid)) >= float(TOPK)
        return jnp.where(ok, mid, lo), jnp.where(ok, hi, mid)

    thr, _ = lax.fori_loop(0, BISECT_ITERS, bis_body, (lo, hi))
    m_ref[...] = jnp.where(sc_ref[...] >= thr, 0.0, -inf)
    mn_ref[...] = jnp.where(scn_ref[...] >= thr, 0.0, -inf)


def _dsa_s_mask(sc, scn):
    rows, past = sc.shape
    return pl.pallas_call(
        _dsa_s_mask_kernel,
        out_shape=[jax.ShapeDtypeStruct((rows, past), F32), jax.ShapeDtypeStruct((rows, PAGE), F32)],
        compiler_params=_params(None),
        name="dsa_sample_mask",
    )(sc, scn)


def _dsa_s_attn_kernel(pt_ref, q_ref, m_ref, mn_ref, kn_ref, vn_ref, hm_ref, bias_ref, *refs, n_steps):
    kp = refs[:S3_PAGES]
    vp = refs[S3_PAGES:2 * S3_PAGES]
    o_ref, acc_ref, mx_ref, l_ref = refs[2 * S3_PAGES:]
    j = pl.program_id(1)
    ninf = jnp.float32(-jnp.inf)
    nrow = SA_HEADS * DEC_T
    pcols = PAGE * SA_KV_HEADS
    q = q_ref[0]

    def update(s, v_bf16, first):
        m_old = jnp.full((nrow, 1), ninf, F32) if first else mx_ref[:, 0:1]
        m_new = jnp.maximum(m_old, s.max(axis=1, keepdims=True))
        m_safe = jnp.where(m_new == ninf, 0.0, m_new)
        p = jnp.exp2(s - m_safe)
        pv = _dot(p.astype(BF16), v_bf16)
        if first:
            l_new = p.sum(axis=1, keepdims=True)
            acc_new = pv
        else:
            alpha = jnp.exp2(m_old - m_safe)
            l_new = l_ref[:, 0:1] * alpha + p.sum(axis=1, keepdims=True)
            acc_new = acc_ref[...] * alpha + pv
        mx_ref[...] = jnp.broadcast_to(m_new, (nrow, 128))
        l_ref[...] = jnp.broadcast_to(l_new, (nrow, 128))
        acc_ref[...] = acc_new

    def expand(mask_t):
        return jnp.concatenate([mask_t[0:DEC_T]] * SA_HEADS, axis=0)

    hm = hm_ref[...]

    @pl.when(j == 0)
    def _():
        s = _dot_nt(q, kn_ref[0]) + bias_ref[:, pcols:pcols + PAGE] + hm[:, :PAGE] + expand(mn_ref[0])
        update(s, vn_ref[0], True)

    is_last = j == n_steps - 1
    k = jnp.concatenate([kp[i][0] for i in range(S3_PAGES)], axis=0).astype(BF16)
    v = jnp.concatenate([vp[i][0] for i in range(S3_PAGES)], axis=0).astype(BF16)
    tail = jnp.where(is_last, bias_ref[:, 0:pcols], 0.0)
    bias = jnp.concatenate([hm] * (S3_PAGES - 1) + [hm + tail], axis=1)
    s = _dot_nt(q, k) + bias + expand(m_ref[0])
    update(s, v, False)

    @pl.when(is_last)
    def _():
        o_ref[0] = acc_ref[...] / l_ref[...]


def _dsa_s_attn(page_table, q_s, madd_x, madd_new_x, k_new, v_new, head_mask, bias, cache_k, cache_v):
    dbsz, n_pages = page_table.shape
    n_steps = n_pages // S3_PAGES
    nrow = SA_HEADS * DEC_T
    pcols = PAGE * SA_KV_HEADS
    page_spec = lambda i: pl.BlockSpec((1, pcols, SA_DIM), lambda b, j, pt: (pt[b * n_pages + j * S3_PAGES + i], 0, 0))
    grid_spec = pltpu.PrefetchScalarGridSpec(
        num_scalar_prefetch=1,
        grid=(dbsz, n_steps),
        in_specs=[pl.BlockSpec((1, nrow, SA_DIM), lambda b, j, pt: (b, 0, 0)),
                  pl.BlockSpec((1, 8, S3_PAGES * pcols), lambda b, j, pt: (b, 0, j)),
                  pl.BlockSpec((1, 8, PAGE), lambda b, j, pt: (b, 0, 0)),
                  pl.BlockSpec((1, PAGE, SA_DIM), lambda b, j, pt: (b, 0, 0)),
                  pl.BlockSpec((1, PAGE, SA_DIM), lambda b, j, pt: (b, 0, 0)),
                  pl.BlockSpec((nrow, pcols), lambda b, j, pt: (0, 0)),
                  pl.BlockSpec((nrow, pcols + PAGE), lambda b, j, pt: (0, 0))]
                 + [page_spec(i) for i in range(S3_PAGES)] * 2,
        out_specs=pl.BlockSpec((1, nrow, SA_DIM), lambda b, j, pt: (b, 0, 0)),
        scratch_shapes=[pltpu.VMEM((nrow, SA_DIM), F32), pltpu.VMEM((nrow, 128), F32), pltpu.VMEM((nrow, 128), F32)],
    )
    return pl.pallas_call(
        functools.partial(_dsa_s_attn_kernel, n_steps=n_steps),
        grid_spec=grid_spec,
        out_shape=jax.ShapeDtypeStruct((dbsz, nrow, SA_DIM), F32),
        compiler_params=_params(("parallel", "arbitrary")),
        name="dsa_sample_attn",
    )(page_table.reshape(-1), q_s, madd_x, madd_new_x, k_new, v_new, head_mask, bias,
      *([cache_k] * S3_PAGES), *([cache_v] * S3_PAGES))


def _dsa_sample(qn, ko, vo, qi, kio, wit, cache_k, cache_v, cache_kidx_t, page_table, rel_bias):
    dbsz, n_pages = page_table.shape
    t = DEC_T
    nkv = SA_KV_HEADS
    past = n_pages * PAGE
    assert n_pages % S1_PAGES == 0 and n_pages % S3_PAGES == 0 and PAGE >= REL_MAX_DIST and PAGE >= nkv * t
    n_pool = cache_k.shape[0]
    qi_s = qi.reshape(IDX_HEADS, dbsz, t, IDX_DIM).transpose(1, 2, 0, 3).reshape(dbsz, t * IDX_HEADS, IDX_DIM)
    w_bth = wit.reshape(IDX_HEADS, dbsz, t).transpose(1, 2, 0)
    wsel = (w_bth[:, :, None, :] * jnp.eye(t, dtype=F32)[None, :, :, None]).reshape(dbsz, t, t * IDX_HEADS)
    wsel = jnp.pad(wsel, ((0, 0), (0, 8 - t), (0, 0)))
    ki_new_t = jnp.pad(kio.reshape(dbsz, t, IDX_DIM).transpose(0, 2, 1), ((0, 0), (0, 0), (0, PAGE - t)))
    sc, scn = _dsa_s_scores(page_table, qi_s, wsel, ki_new_t, cache_kidx_t)
    madd, madd_new = _dsa_s_mask(sc.reshape(dbsz * 8, past), scn.reshape(dbsz * 8, PAGE))
    madd_x = jnp.repeat(madd.reshape(dbsz, 8, past), nkv, axis=-1)
    madd_new_x = jnp.repeat(madd_new.reshape(dbsz, 8, PAGE)[:, :, :PAGE // nkv], nkv, axis=-1)
    new_rows = lambda a: jnp.pad(a.reshape(dbsz, t * nkv, SA_DIM), ((0, 0), (0, PAGE - t * nkv), (0, 0))).astype(BF16)
    q_s = qn.reshape(nkv, SA_GROUP, dbsz, t, SA_DIM).transpose(2, 0, 1, 3, 4).reshape(dbsz, SA_HEADS * t, SA_DIM)
    nrow = SA_HEADS * t
    r = jnp.arange(nrow, dtype=jnp.int32)
    c = jnp.arange(PAGE * nkv, dtype=jnp.int32)
    head_mask = jnp.where((c[None, :] % nkv) == (r[:, None] // (SA_GROUP * t)), 0.0, -jnp.inf).astype(F32)
    dist = jnp.concatenate([PAGE + (r % t)[:, None] - (c // nkv)[None, :],
                            (r % t)[:, None] - (c[:PAGE] // nkv)[None, :]], axis=1)
    rb_rows = jnp.repeat(rel_bias.astype(F32).T, t, axis=0)
    onehot = _t5_bucket(dist)[:, :, None] == jnp.arange(REL_BUCKETS, dtype=jnp.int32)
    bias = (jnp.sum(jnp.where(onehot, rb_rows[:, None, :], 0.0), axis=-1) - rb_rows[:, REL_BUCKETS - 1:]) * LOG2E
    o = _dsa_s_attn(page_table, q_s, madd_x, madd_new_x, new_rows(ko), new_rows(vo), head_mask, bias,
                    cache_k.reshape(n_pool, PAGE * nkv, SA_DIM), cache_v.reshape(n_pool, PAGE * nkv, SA_DIM))
    return o.reshape(dbsz, nkv, SA_GROUP, t, SA_DIM).transpose(0, 3, 1, 2, 4).reshape(dbsz * t, SA_HEADS * SA_DIM)


def _merge_kernel(odn_ref, osa_ref, gd_ref, gs_ref, w1_ref, w2_ref, o_ref):
    a = _dot(odn_ref[...], w1_ref[...])
    b = _dot(osa_ref[...], w2_ref[...])
    o_ref[...] = (jax.nn.sigmoid(gd_ref[...]) * a + jax.nn.sigmoid(gs_ref[...]) * b).astype(o_ref.dtype)


def _merge(o_dn, o_sa, p, w1, w2, tn=512):
    m, k = o_dn.shape
    n = w1.shape[1]
    tm = _row_tile(m)
    return pl.pallas_call(
        _merge_kernel,
        grid=(n // tn, m // tm),
        in_specs=[pl.BlockSpec((tm, k), lambda j, i: (i, 0)),
                  pl.BlockSpec((tm, k), lambda j, i: (i, 0)),
                  pl.BlockSpec((tm, tn), lambda j, i: (i, C_GD // tn + j)),
                  pl.BlockSpec((tm, tn), lambda j, i: (i, C_GS // tn + j)),
                  pl.BlockSpec((k, tn), lambda j, i: (0, j)),
                  pl.BlockSpec((k, tn), lambda j, i: (0, j))],
        out_specs=pl.BlockSpec((tm, tn), lambda j, i: (i, j)),
        out_shape=jax.ShapeDtypeStruct((m, n), BF16),
        compiler_params=_params(("parallel", "parallel")),
        name="merge",
    )(o_dn, o_sa, p, p, w1, w2)


def _pack_bf16_pair(lo, hi):
    lo_bits = pltpu.bitcast(lo.astype(BF16).astype(F32), jnp.uint32)
    hi_bits = pltpu.bitcast(hi.astype(BF16).astype(F32), jnp.uint32)
    return (hi_bits & jnp.uint32(0xFFFF0000)) | (lo_bits >> 16)


def _unpack_bf16_pair(w):
    lo = pltpu.bitcast(w << 16, F32)
    hi = pltpu.bitcast(w & jnp.uint32(0xFFFF0000), F32)
    return jnp.concatenate([lo, hi], axis=1).astype(BF16)


def _post_attn_kernel(xp_ref, xs_ref, m_ref, g1_ref, sh_ref, sc_ref, gn_ref, wr_ref, br_ref, x1_ref, hp_ref, lg_ref,
                      *, n_prompt_tiles):
    x = jnp.where(pl.program_id(0) < n_prompt_tiles, xp_ref[...], xs_ref[...])
    x1 = x + g1_ref[...] * m_ref[...]
    x1_ref[...] = x1
    y = x1 * lax.rsqrt(jnp.mean(x1 * x1, axis=-1, keepdims=True) + EPS) * gn_ref[...]
    h2 = y * (1.0 + sc_ref[...]) + sh_ref[...]
    half = h2.shape[1] // 2
    hp_ref[...] = _pack_bf16_pair(h2[:, :half], h2[:, half:])
    lg_ref[...] = lax.dot_general(wr_ref[...], h2, (((1,), (1,)), ((), ())), precision=HI,
                                  preferred_element_type=F32) + br_ref[...]


def _post_attn(x_p, x_s, m2, mod_exp, g_norm2, w_router_t, b_router, n_prompt_tiles):
    t, d = m2.shape
    rb = _mod_row_block(n_prompt_tiles)
    tile = lambda c: pl.BlockSpec((TOK_TILE, d), lambda i: (rb(i), c))
    return pl.pallas_call(
        functools.partial(_post_attn_kernel, n_prompt_tiles=n_prompt_tiles),
        grid=(t // TOK_TILE,),
        in_specs=_token_specs(d, n_prompt_tiles) + [
            pl.BlockSpec((TOK_TILE, d), lambda i: (i, 0)),
            tile(2), tile(3), tile(4),
            pl.BlockSpec((1, d), lambda i: (0, 0)),
            pl.BlockSpec((N_EXPERTS, d), lambda i: (0, 0)),
            pl.BlockSpec((N_EXPERTS, 1), lambda i: (0, 0))],
        out_specs=[pl.BlockSpec((TOK_TILE, d), lambda i: (i, 0)),
                   pl.BlockSpec((TOK_TILE, d // 2), lambda i: (i, 0)),
                   pl.BlockSpec((N_EXPERTS, TOK_TILE), lambda i: (0, i))],
        out_shape=[jax.ShapeDtypeStruct((t, d), F32),
                   jax.ShapeDtypeStruct((t, d // 2), jnp.uint32),
                   jax.ShapeDtypeStruct((N_EXPERTS, t), F32)],
        compiler_params=_params(("parallel",)),
        name="post_attn",
    )(x_p, x_s, m2, mod_exp, mod_exp, mod_exp, g_norm2.reshape(1, d), w_router_t, b_router.reshape(N_EXPERTS, 1))


def _route_kernel(lg_ref, e_ref, w_ref, r_ref, cnt_ref, run_ref):
    @pl.when(pl.program_id(0) == 0)
    def _():
        run_ref[...] = jnp.zeros(run_ref.shape, F32)

    tt = TOK_TILE
    lg = lg_ref[...]
    e_iota = lax.broadcasted_iota(jnp.int32, lg.shape, 0)
    sels, tops = [], []
    for r in range(TOP_K):
        m = lg.max(axis=0, keepdims=True)
        e = jnp.where(lg == m, e_iota, N_EXPERTS).min(axis=0, keepdims=True)
        sel = e_iota == e
        lg = jnp.where(sel, -jnp.inf, lg)
        e_ref[r:r + 1, :] = e
        sels.append(sel)
        tops.append(m)
    ex = [jnp.exp(m - tops[0]) for m in tops]
    denom = ex[0] + ex[1] + ex[2] + ex[3]
    for r in range(TOP_K):
        w_ref[r:r + 1, :] = ex[r] / denom
    onehot = sels[0] | sels[1] | sels[2] | sels[3]
    oh = jnp.where(onehot, 1.0, 0.0).astype(BF16)
    before = (lax.broadcasted_iota(jnp.int32, (tt, tt), 0) < lax.broadcasted_iota(jnp.int32, (tt, tt), 1))
    prefix = _dot(oh, jnp.where(before, 1.0, 0.0).astype(BF16))
    total = _dot(oh, jnp.ones((tt, tt), BF16))
    base = run_ref[...] + prefix
    for r in range(TOP_K):
        r_ref[r:r + 1, :] = jnp.where(sels[r], base, 0.0).sum(axis=0, keepdims=True).astype(jnp.int32)
    run_ref[...] = run_ref[...] + total
    cnt_ref[...] = run_ref[...]


def _route(logits_t):
    e, t = logits_t.shape
    spec4 = pl.BlockSpec((TOP_K, TOK_TILE), lambda i: (0, i))
    return pl.pallas_call(
        _route_kernel,
        grid=(t // TOK_TILE,),
        in_specs=[pl.BlockSpec((e, TOK_TILE), lambda i: (0, i))],
        out_specs=[spec4, spec4, spec4, pl.BlockSpec((e, TOK_TILE), lambda i: (0, 0))],
        out_shape=[jax.ShapeDtypeStruct((TOP_K, t), jnp.int32),
                   jax.ShapeDtypeStruct((TOP_K, t), F32),
                   jax.ShapeDtypeStruct((TOP_K, t), jnp.int32),
                   jax.ShapeDtypeStruct((e, TOK_TILE), F32)],
        scratch_shapes=[pltpu.VMEM((e, TOK_TILE), F32)],
        compiler_params=_params(("arbitrary",)),
        name="route",
    )(logits_t)


def _dispatch_kernel(idx_ref, h_ref, o_ref):
    def body(r, carry):
        o_ref[pl.ds(r, 1), :] = h_ref[pl.ds(idx_ref[r], 1), :]
        return carry

    lax.fori_loop(0, MOE_BLOCK, body, 0, unroll=8)


def _dispatch(row_tok, h2p):
    rows = row_tok.shape[0]
    t, w = h2p.shape
    return pl.pallas_call(
        _dispatch_kernel,
        grid=(rows // MOE_BLOCK,),
        in_specs=[pl.BlockSpec((MOE_BLOCK,), lambda i: (i,), memory_space=pltpu.SMEM),
                  pl.BlockSpec((t, w), lambda i: (0, 0), pipeline_mode=pl.Buffered(1))],
        out_specs=pl.BlockSpec((MOE_BLOCK, w), lambda i: (i, 0)),
        out_shape=jax.ShapeDtypeStruct((rows, w), jnp.uint32),
        compiler_params=_params(("parallel",)),
        name="moe_dispatch",
    )(row_tok, h2p)


MOE_GROUP = 6
FF_TILE = 512
N_FF = D_FF // FF_TILE
ST_BLK, ST_F, ST_E, ST_J, ST_CAST, ST_VALID, ST_DONE = range(7)


def _expert_kernel(st_ref, dest_ref, x_ref, wg_ref, wu_ref, wd_ref, bg_ref, bu_ref, bd_ref, ys_ref,
                   wgb_ref, wub_ref, wdb_ref, acc_ref, stage_ref, sem_ref, inflight_ref):
    s = pl.program_id(0)
    f = st_ref[ST_F, s]
    j = st_ref[ST_J, s]

    def row_copy(slot, r, dst_row):
        return pltpu.make_async_copy(stage_ref.at[slot, pl.ds(r, 1)], ys_ref.at[pl.ds(dst_row, 1)], sem_ref.at[slot])

    def drain(slot):
        def body(r, carry):
            row_copy(slot, 0, 0).wait()
            return carry
        lax.fori_loop(0, inflight_ref[slot], body, 0)
        inflight_ref[slot] = 0

    @pl.when(s == 0)
    def _():
        inflight_ref[0] = 0
        inflight_ref[1] = 0

    @pl.when(st_ref[ST_CAST, s] == 1)
    def _():
        wgb_ref[...] = wg_ref[0].astype(BF16)
        wub_ref[...] = wu_ref[0].astype(BF16)
        wdb_ref[...] = wd_ref[0].astype(BF16)

    @pl.when(st_ref[ST_VALID, s] == 1)
    def _():
        x = _unpack_bf16_pair(x_ref[...])
        gate = jnp.minimum(_dot(x, wgb_ref[...]) + bg_ref[0], SWIGLU_LIMIT)
        up = jnp.clip(_dot(x, wub_ref[...]) + bu_ref[0], -SWIGLU_LIMIT, SWIGLU_LIMIT)
        act = (up + 1.0) * gate * jax.nn.sigmoid(SWIGLU_ALPHA * gate)
        part = _dot(act.astype(BF16), wdb_ref[...])

        @pl.when(f == 0)
        def _():
            acc_ref[j] = part

        @pl.when(f > 0)
        def _():
            acc_ref[j] = acc_ref[j] + part

        @pl.when(f == N_FF - 1)
        def _():
            slot = st_ref[ST_DONE, s] % 2
            drain(slot)
            stage_ref[slot] = acc_ref[j] + bd_ref[0]

            def body(r, n):
                dst = dest_ref[r]

                @pl.when(dst >= 0)
                def _():
                    row_copy(slot, r, dst).start()
                return n + jnp.where(dst >= 0, 1, 0)
            inflight_ref[slot] = lax.fori_loop(0, MOE_BLOCK, body, 0, unroll=8)

    @pl.when(s == pl.num_programs(0) - 1)
    def _():
        drain(0)
        drain(1)


def _experts(steps, dest_row, xg, w_up, b_up, w_down, b_down, n_out_rows):
    n_steps = steps.shape[1]
    rows, wpk = xg.shape
    d = w_down.shape[2]
    e = w_up.shape[0]
    grid_spec = pltpu.PrefetchScalarGridSpec(
        num_scalar_prefetch=1,
        grid=(n_steps,),
        in_specs=[pl.BlockSpec((MOE_BLOCK,), lambda s, st: (st[ST_BLK, s],), memory_space=pltpu.SMEM),
                  pl.BlockSpec((MOE_BLOCK, wpk), lambda s, st: (st[ST_BLK, s], 0)),
                  pl.BlockSpec((1, d, FF_TILE), lambda s, st: (st[ST_E, s], 0, st[ST_F, s])),
                  pl.BlockSpec((1, d, FF_TILE), lambda s, st: (st[ST_E, s], 0, N_FF + st[ST_F, s])),
                  pl.BlockSpec((1, FF_TILE, d), lambda s, st: (st[ST_E, s], st[ST_F, s], 0)),
                  pl.BlockSpec((1, 1, FF_TILE), lambda s, st: (st[ST_E, s], 0, st[ST_F, s])),
                  pl.BlockSpec((1, 1, FF_TILE), lambda s, st: (st[ST_E, s], 0, N_FF + st[ST_F, s])),
                  pl.BlockSpec((1, 1, d), lambda s, st: (st[ST_E, s], 0, 0))],
        out_specs=pl.BlockSpec(memory_space=pl.ANY),
        scratch_shapes=[pltpu.VMEM((d, FF_TILE), BF16), pltpu.VMEM((d, FF_TILE), BF16), pltpu.VMEM((FF_TILE, d), BF16),
                        pltpu.VMEM((MOE_GROUP, MOE_BLOCK, d), F32),
                        pltpu.VMEM((2, MOE_BLOCK, d), F32),
                        pltpu.SemaphoreType.DMA((2,)),
                        pltpu.SMEM((2,), jnp.int32)],
    )
    return pl.pallas_call(
        _expert_kernel,
        grid_spec=grid_spec,
        out_shape=jax.ShapeDtypeStruct((n_out_rows, d), F32),
        compiler_params=_params(("arbitrary",)),
        name="moe_experts",
    )(steps, dest_row, xg, w_up, w_up, w_down, b_up.reshape(e, 1, -1), b_up.reshape(e, 1, -1), b_down.reshape(e, 1, d))


def _moe_plan(top_e, rank, counts, t):
    blk = MOE_BLOCK
    n_assign = TOP_K * t
    n_blocks = -(-n_assign // blk) + N_EXPERTS
    rows = n_blocks * blk
    nblk_e = (counts + blk - 1) // blk
    blk_end = jnp.cumsum(nblk_e)
    blk_start = blk_end - nblk_e
    used = blk_end[-1]
    e_ids = jnp.arange(N_EXPERTS, dtype=jnp.int32)
    start_of = jnp.sum(jnp.where(top_e[:, :, None] == e_ids, blk_start, 0), axis=-1)
    dest = start_of * blk + rank
    flat_slot_tok = (jnp.arange(TOP_K, dtype=jnp.int32)[:, None] * t + jnp.arange(t, dtype=jnp.int32)[None, :])
    dest_row = jnp.full((rows,), -1, jnp.int32).at[dest.reshape(-1)].set(flat_slot_tok.reshape(-1))
    row_tok = jnp.maximum(dest_row, 0) % t
    b = jnp.arange(n_blocks, dtype=jnp.int32)
    e_b = jnp.minimum(jnp.sum(blk_end[None, :] <= b[:, None], axis=1), N_EXPERTS - 1).astype(jnp.int32)
    lb = b - blk_start[e_b]
    j_b = lb % MOE_GROUP
    gsize = jnp.minimum(MOE_GROUP, nblk_e[e_b] - (lb - j_b))
    valid_b = b < used
    fidx = jnp.arange(N_FF, dtype=jnp.int32)
    step_of = N_FF * (b - j_b)[:, None] + fidx[None, :] * gsize[:, None] + j_b[:, None]
    n_steps = N_FF * n_blocks
    step_of = jnp.where(valid_b[:, None], step_of, n_steps)
    def scat(vals, fill):
        return jnp.full((n_steps,), fill, jnp.int32).at[step_of.reshape(-1)].set(
            jnp.broadcast_to(vals, (n_blocks, N_FF)).reshape(-1).astype(jnp.int32), mode="drop")
    st_valid = scat(jnp.ones((n_blocks, 1), jnp.int32), 0)
    st_blk = scat(b[:, None], -1)
    st_f = scat(fidx[None, :], -1)
    st_e = scat(e_b[:, None], -1)
    st_j = scat(j_b[:, None], 0)
    st_cast = scat((j_b == 0)[:, None], 0)
    n_valid_steps = N_FF * used
    last_idx = jnp.maximum(n_valid_steps - 1, 0)
    pad = jnp.arange(n_steps) >= n_valid_steps
    fix = lambda a: jnp.where(pad, a[last_idx], a)
    st_blk, st_f, st_e = fix(st_blk), fix(st_f), fix(st_e)
    finishing = (st_valid == 1) & (st_f == N_FF - 1)
    st_done = jnp.cumsum(finishing.astype(jnp.int32)) - finishing.astype(jnp.int32)
    steps = jnp.stack([st_blk, st_f, st_e, st_j, st_cast, st_valid, st_done]).astype(jnp.int32)
    return steps, row_tok, dest_row, rows


def _final_kernel(x1_ref, g2_ref, w_ref, y0_ref, y1_ref, y2_ref, y3_ref, op_ref, os_ref, *, n_prompt_tiles):
    w = w_ref[...]
    moe = (w[:, 0:1] * y0_ref[...] + w[:, 1:2] * y1_ref[...]) + (w[:, 2:3] * y2_ref[...] + w[:, 3:4] * y3_ref[...])
    y = x1_ref[...] + g2_ref[...] * moe
    i = pl.program_id(0)

    @pl.when(i < n_prompt_tiles)
    def _():
        op_ref[...] = y

    @pl.when(i >= n_prompt_tiles)
    def _():
        os_ref[...] = y


def _final(x1, mod_exp, top_w_t, ys, n_prompt_tiles):
    t, d = x1.shape
    rb = _mod_row_block(n_prompt_tiles)
    nt = t // TOK_TILE
    slot = lambda k: pl.BlockSpec((TOK_TILE, d), lambda i: (k * nt + i, 0))
    return pl.pallas_call(
        functools.partial(_final_kernel, n_prompt_tiles=n_prompt_tiles),
        grid=(nt,),
        in_specs=[pl.BlockSpec((TOK_TILE, d), lambda i: (i, 0)),
                  pl.BlockSpec((TOK_TILE, d), lambda i: (rb(i), 5)),
                  pl.BlockSpec((TOK_TILE, TOP_K), lambda i: (i, 0)),
                  slot(0), slot(1), slot(2), slot(3)],
        out_specs=_token_specs(d, n_prompt_tiles),
        out_shape=[jax.ShapeDtypeStruct((n_prompt_tiles * TOK_TILE, d), F32),
                   jax.ShapeDtypeStruct(((nt - n_prompt_tiles) * TOK_TILE, d), F32)],
        compiler_params=_params(("arbitrary",)),
        name="moe_combine",
    )(x1, mod_exp, top_w_t, ys, ys, ys, ys)


def _moe(h2p, logits_t, w_up, b_up, w_down, b_down):
    t = h2p.shape[0]
    top_e, top_w, rank, cnt = _route(logits_t)
    counts = cnt[:, 0].astype(jnp.int32)
    steps, row_tok, dest_row, rows = _moe_plan(top_e, rank, counts, t)
    xg = _dispatch(row_tok, h2p)
    ys = _experts(steps, dest_row, xg, w_up, b_up, w_down, b_down, TOP_K * t)
    return ys, top_w.T


PROJ_SIZES = (DN_CONV_CH, DN_HEADS * DN_DIM, DN_HEADS, DN_HEADS, SA_HEADS * SA_DIM, SA_KV_HEADS * SA_DIM,
              SA_KV_HEADS * SA_DIM, IDX_HEADS * IDX_DIM, IDX_DIM, IDX_HEADS, D_MODEL, D_MODEL)


def _split_w_in(w_in_t):
    ends = np.cumsum(PROJ_SIZES)
    seg = [w_in_t[int(e - s):int(e)] for s, e in zip(PROJ_SIZES, ends)]
    (dn_qkv, dn_z, dn_a, dn_b, sa_q, sa_k, sa_v, ix_q, ix_k, ix_w, gate_dn, gate_sa) = seg
    w_big = jnp.concatenate([dn_qkv, dn_z, sa_q, sa_k, sa_v, ix_q, gate_dn, gate_sa], axis=0).astype(BF16)
    pad = jnp.zeros((128 - (IDX_DIM + 2 * DN_HEADS + IDX_HEADS), w_in_t.shape[1]), w_in_t.dtype)
    w_small = jnp.concatenate([ix_k, dn_a, dn_b, ix_w, pad], axis=0)
    return w_big, w_small


def _pad_seq(a, bsz, t, t_pad, front=0):
    a = a.reshape(bsz, t, -1)
    return jnp.pad(a, ((0, 0), (front, t_pad - t - front), (0, 0))).reshape(bsz * t_pad, -1)


def kernel(x_prompt, x_sample, cache_k, cache_v, cache_kidx, state_conv, state_ssm, page_table, c_prompt, c_sample,
           rel_bias, w_ada, b_ada, g_norm1, w_in, conv_w, a_log, dt_bias, g_dn_out, g_q, g_k, w_o_dn, w_o_sa, w_out,
           g_norm2, w_router, b_router, w_up, b_up, w_down, b_down):
    assert w_ada.shape[0] == 1 and x_prompt.shape[0] == 1
    d = D_MODEL
    tp = x_prompt.shape[1]
    dbsz, dt = x_sample.shape[:2]
    ts = dbsz * dt
    assert dt == DEC_T and ts == TOK_TILE and tp % DSA_TQ == 0
    npt = tp // TOK_TILE
    x_p = x_prompt.reshape(tp, d)
    x_s = x_sample.reshape(ts, d)

    c_all = jnp.concatenate([c_prompt, c_sample], axis=0)
    c_all = jnp.pad(c_all, ((0, -c_all.shape[0] % 8), (0, 0)))
    mod = _adaln(c_all, w_ada[0], b_ada[0])
    mod_exp = jnp.concatenate([jnp.broadcast_to(mod[0:1], (TOK_TILE, mod.shape[1])),
                               jnp.repeat(mod[1:1 + dbsz], dt, axis=0)], axis=0)

    w_big_t, w_small_t = _split_w_in(jnp.swapaxes(w_in, 1, 2)[0])
    h, small = _norm_mod(x_p, x_s, g_norm1[0], mod_exp, w_small_t, npt)
    p = _matmul(h, w_big_t, F32, w_transposed=True)

    qn, ko_p, kb, vo_p, vt, qi, kio_p, kib, wit = _sa_prep(p, small, g_q[0], g_k[0], 0, tp, DSA_TQ)
    o_sa_p = _dsa_prompt(qn, kb, vt, qi, kib, wit, rel_bias)
    qn_s, ko_s, _, vo_s, _, qi_s, kio_s, _, wit_s = _sa_prep(p, small, g_q[0], g_k[0], tp, ts, TOK_TILE)
    n_pool = cache_k.shape[1]
    o_sa_s = _dsa_sample(qn_s, ko_s, vo_s, qi_s, kio_s, wit_s, cache_k.reshape(n_pool, *cache_k.shape[2:]),
                         cache_v.reshape(n_pool, *cache_v.shape[2:]),
                         jnp.swapaxes(cache_kidx, 2, 3).reshape(n_pool, IDX_DIM, PAGE), page_table, rel_bias)

    tm = 256
    xc, beta, g = _dn_prep(p, C_QKV, p, lambda i: (jnp.maximum(i * (tm // 8) - 1, 0), 0), small[:tp],
                           conv_w[0], a_log[0], dt_bias[0], tm=tm, n_valid=tm, zero_first_halo=True)
    chunk = 64
    s0_p = jnp.zeros((1,) + state_ssm.shape[2:], F32)
    o_dn_p, ssm_p = _dn_scan(xc, p, C_Z, beta, g, s0_p, g_dn_out[0], bsz=1, n_chunks=tp // chunk, c=chunk)
    qkv_s = p[tp:, C_QKV:C_QKV + DN_CONV_CH]
    halo_s = _pad_seq(state_conv[0].reshape(dbsz * (CONV_WIDTH - 1), -1), dbsz, CONV_WIDTH - 1, 8, front=8 - (CONV_WIDTH - 1))
    xc_s, beta_s, g_s = _dn_prep(_pad_seq(qkv_s, dbsz, dt, 8), 0, halo_s, lambda i: (i, 0), _pad_seq(small[tp:], dbsz, dt, 8),
                                 conv_w[0], a_log[0], dt_bias[0], tm=8, n_valid=dt, zero_first_halo=False)
    z_s = _pad_seq(p[tp:, C_Z:C_Z + DN_HEADS * DN_DIM], dbsz, dt, 8)
    o_dn_s, ssm_s = _dn_scan(xc_s, z_s, 0, beta_s, g_s, state_ssm[0], g_dn_out[0], bsz=dbsz, n_chunks=1, c=8)
    o_dn_s = o_dn_s.reshape(dbsz, 8, -1)[:, :dt].reshape(ts, -1)
    conv_p = p[tp - (CONV_WIDTH - 1):tp, C_QKV:C_QKV + DN_CONV_CH]
    conv_s = jnp.concatenate([state_conv[0], qkv_s.reshape(dbsz, dt, -1)], axis=1)[:, -(CONV_WIDTH - 1):]

    o_dn = jnp.concatenate([o_dn_p, o_dn_s], axis=0)
    o_sa = jnp.concatenate([o_sa_p, o_sa_s.astype(BF16)], axis=0)
    merged = _merge(o_dn, o_sa, p, w_o_dn[0].astype(BF16), w_o_sa[0].astype(BF16))
    m2 = _matmul(merged, w_out[0].astype(BF16), F32)
    x1, h2p, logits_t = _post_attn(x_p, x_s, m2, mod_exp, g_norm2[0], jnp.swapaxes(w_router, 1, 2)[0], b_router[0], npt)
    ys, top_w_t = _moe(h2p, logits_t, w_up[0], b_up[0], w_down[0], b_down[0])
    y_p, y_s = _final(x1, mod_exp, top_w_t, ys, npt)

    kv = (SA_KV_HEADS, SA_DIM)
    return (y_p.reshape(x_prompt.shape), y_s.reshape(x_sample.shape),
            ko_p.reshape(1, 1, tp, *kv), vo_p.reshape(1, 1, tp, *kv), kio_p.reshape(1, 1, tp, IDX_DIM),
            conv_p.reshape(1, 1, CONV_WIDTH - 1, DN_CONV_CH), ssm_p.reshape(1, *ssm_p.shape),
            ko_s.reshape(1, dbsz, dt, *kv), vo_s.reshape(1, dbsz, dt, *kv), kio_s.reshape(1, dbsz, dt, IDX_DIM),
            conv_s.reshape(1, dbsz, CONV_WIDTH - 1, DN_CONV_CH), ssm_s.reshape(1, *ssm_s.shape))
```
